```python
import math
import jax, jax.numpy as jnp
from jax import lax
import numpy as np

D_MODEL = 1024
BATCH = 1
SEQ = 16384
DEPTH = 1
DEC_BATCH = 32
DEC_SEQ = 4
PAST_LEN = 16384
PAGE_SIZE = 128

HEAD_DIM = 64
NSA_HEADS = 8
NSA_KV_HEADS = 2
NSA_GROUP = NSA_HEADS // NSA_KV_HEADS
FOX_HEADS = 8
ROT_DIM = HEAD_DIM // 4
ROPE_THETA = 500000.0
CMP_BLOCK = 32
CMP_STRIDE = 16
CMP_HIDDEN = 128
SEL_BLOCK = 64
N_SEL = 16
WINDOW = 512
Q_BLOCK = 128
D_FF = 2752
EPS = 1e-6
FORCE_SCORE = 1e6
FORGET_BIAS = 3.0
NSA_Q_W = NSA_HEADS * HEAD_DIM
NSA_KV_W = NSA_KV_HEADS * HEAD_DIM
FOX_W = FOX_HEADS * HEAD_DIM
SPLITS = (NSA_Q_W, 6 * NSA_KV_W, 3 * NSA_HEADS, 3 * FOX_W, FOX_HEADS, 2 * D_MODEL)
D_IN = NSA_Q_W + 6 * NSA_KV_W + 3 * NSA_HEADS + 3 * FOX_W + FOX_HEADS + 2 * D_MODEL

kernel_name = 'nsa_fox_gated_macaron_step'


def _rmsnorm(x, g):
    xf = x.astype(jnp.float32)
    y = xf * lax.rsqrt(jnp.mean(xf * xf, axis=-1, keepdims=True) + EPS)
    return (y * g.astype(jnp.float32)).astype(x.dtype)


def _rope(x, pos):
    half = ROT_DIM // 2
    inv = ROPE_THETA ** (-(jnp.arange(half, dtype=jnp.float32) * 2.0 / ROT_DIM))
    ang = pos.astype(jnp.float32)[:, None] * inv[None, :]
    cos = jnp.cos(ang)[:, None, :]
    sin = jnp.sin(ang)[:, None, :]
    xr = x[..., :ROT_DIM].astype(jnp.float32)
    x1, x2 = xr[..., :half], xr[..., half:]
    rot = jnp.concatenate([x1 * cos - x2 * sin, x2 * cos + x1 * sin], axis=-1)
    return jnp.concatenate([rot.astype(x.dtype), x[..., ROT_DIM:]], axis=-1)


def _masked_softmax(s, mask):
    s = jnp.where(mask, s, -jnp.inf)
    m = jnp.max(s, axis=-1, keepdims=True)
    m = jnp.where(jnp.isfinite(m), m, 0.0)
    e = jnp.exp(s - m)
    d = jnp.sum(e, axis=-1, keepdims=True)
    return e / jnp.where(d > 0, d, 1.0)


def _swiglu(x, wg, wu, wd):
    return (jax.nn.silu(x @ wg) * (x @ wu)) @ wd


def _pad_axis1(a, before, after):
    return jnp.pad(a, [(0, 0), (before, after)] + [(0, 0)] * (a.ndim - 2))


def _gather_pages(pool, layer, page_table):
    g = pool[layer, page_table]
    return g.reshape((page_table.shape[0], page_table.shape[1] * g.shape[2]) + g.shape[3:])


def _compress(x_raw, pe, w1, w2):
    B, L, G, D = x_raw.shape
    n_cmp = (L - CMP_BLOCK) // CMP_STRIDE + 1
    r = CMP_BLOCK // CMP_STRIDE
    n_sub = n_cmp + r - 1
    sb = x_raw[:, :n_sub * CMP_STRIDE].reshape(B, n_sub, CMP_STRIDE, G, D)
    blocks = jnp.concatenate([sb[:, i:i + n_cmp] for i in range(r)], axis=2)
    blocks = blocks + pe[:, None, :]
    hid = jax.nn.gelu(jnp.einsum('bnlgd,ldh->bngh', blocks, w1))
    return jnp.einsum('bngh,hd->bngd', hid, w2)


def _nsa_attention(q, gates, nsa_all, win_all, q0, win_k0, k_norm_cmp, pe_k, w1_k, w2_k, pe_v, w1_v, w2_v):
    B, Tq, H, D = q.shape
    G, HPG = NSA_KV_HEADS, NSA_GROUP
    L = nsa_all.shape[1]
    scale = 1.0 / math.sqrt(HEAD_DIM)
    kc = _compress(nsa_all[:, :, 0], pe_k, w1_k, w2_k)
    vc = _compress(nsa_all[:, :, 1], pe_v, w1_v, w2_v)
    n_cmp = kc.shape[1]
    pos_c = jnp.arange(n_cmp) * CMP_STRIDE + (CMP_BLOCK - 1)
    kc = _rope(_rmsnorm(kc, k_norm_cmp), pos_c)
    n_slc = -(-L // SEL_BLOCK)
    subs = SEL_BLOCK // CMP_STRIDE
    n_sub = n_slc * subs
    r = CMP_BLOCK // CMP_STRIDE
    k_sel = min(N_SEL, n_slc)

    def to_blocks(a):
        a = _pad_axis1(a, 0, n_slc * SEL_BLOCK - L)
        return a.reshape(B, n_slc, SEL_BLOCK, G, D).transpose(0, 3, 1, 2, 4)

    ks = to_blocks(nsa_all[:, :, 2])
    vs = to_blocks(nsa_all[:, :, 3])
    qb = min(Q_BLOCK, Tq)
    nb = -(-Tq // qb)
    tp = nb * qb
    span = WINDOW + qb
    kw = _pad_axis1(win_all[:, :, 0], WINDOW, qb)
    vw = _pad_axis1(win_all[:, :, 1], WINDOW, qb)
    qp = _pad_axis1(q, 0, tp - Tq)
    gp = _pad_axis1(gates, 0, tp - Tq)
    bix = jnp.arange(B)[:, None, None, None]
    gix = jnp.arange(G)[None, None, :, None]
    blk = jnp.arange(n_slc)[None, :]

    def block(bi):
        start = bi * qb
        qg = lax.dynamic_slice_in_dim(qp, start, qb, axis=1).reshape(B, qb, G, HPG, D)
        gb = lax.dynamic_slice_in_dim(gp, start, qb, axis=1).reshape(B, qb, G, HPG, 3)
        tpos = q0 + start + jnp.arange(qb)
        s = jnp.einsum('bqghd,bngd->bqghn', qg, kc).astype(jnp.float32) * scale
        p_c = _masked_softmax(s, (pos_c[None, :] <= tpos[:, None])[None, :, None, None, :])
        o_c = jnp.einsum('bqghn,bngd->bqghd', p_c.astype(vc.dtype), vc)
        imp = jnp.pad(p_c.sum(axis=3), ((0, 0), (0, 0), (0, 0), (0, n_sub - n_cmp)))
        sub = imp
        for n in range(1, r):
            sub = sub + jnp.pad(imp[..., :n_sub - n], ((0, 0), (0, 0), (0, 0), (n, 0)))
        p_slc = sub.reshape(B, qb, G, n_slc, subs).sum(axis=-1)
        cur = (tpos // SEL_BLOCK)[:, None]
        valid = blk * SEL_BLOCK <= tpos[:, None]
        force = (blk == 0) | (blk == cur) | (blk == cur - 1)
        score = jnp.where(force[None, :, None, :], FORCE_SCORE,
                          jnp.where(valid[None, :, None, :], p_slc, -jnp.inf))
        _, idx = lax.top_k(score, k_sel)
        ksel = ks[bix, gix, idx].reshape(B, qb, G, k_sel * SEL_BLOCK, D)
        vsel = vs[bix, gix, idx].reshape(B, qb, G, k_sel * SEL_BLOCK, D)
        kpos = (idx[..., None] * SEL_BLOCK + jnp.arange(SEL_BLOCK)).reshape(B, qb, G, k_sel * SEL_BLOCK)
        s = jnp.einsum('bqghd,bqgmd->bqghm', qg, ksel).astype(jnp.float32) * scale
        p_s = _masked_softmax(s, (kpos <= tpos[None, :, None, None])[:, :, :, None, :])
        o_s = jnp.einsum('bqghm,bqgmd->bqghd', p_s.astype(vsel.dtype), vsel)
        off = q0 - win_k0 + start
        kwb = lax.dynamic_slice_in_dim(kw, off, span, axis=1)
        vwb = lax.dynamic_slice_in_dim(vw, off, span, axis=1)
        kpos_w = q0 + start - WINDOW + jnp.arange(span)
        mw = ((kpos_w[None, :] <= tpos[:, None]) & (tpos[:, None] - kpos_w[None, :] < WINDOW)
              & (kpos_w[None, :] >= win_k0))
        s = jnp.einsum('bqghd,bkgd->bqghk', qg, kwb).astype(jnp.float32) * scale
        p_w = _masked_softmax(s, mw[None, :, None, None, :])
        o_w = jnp.einsum('bqghk,bkgd->bqghd', p_w.astype(vwb.dtype), vwb)
        o = gb[..., 0:1] * o_c + gb[..., 1:2] * o_s + gb[..., 2:3] * o_w
        return o.reshape(B, qb, H * D)

    out = lax.map(block, jnp.arange(nb))
    return out.transpose(1, 0, 2, 3).reshape(B, tp, H * D)[:, :Tq]


def _fox_attention(q, k, v, c, q0):
    B, Tq, H, D = q.shape
    L = k.shape[1]
    scale = 1.0 / math.sqrt(HEAD_DIM)
    qb = min(Q_BLOCK, Tq)
    nb = -(-Tq // qb)
    tp = nb * qb
    qp = _pad_axis1(q, 0, tp - Tq)
    cq = _pad_axis1(c[:, q0:q0 + Tq], 0, tp - Tq)
    ct = c.transpose(0, 2, 1)
    kpos = jnp.arange(L)

    def block(bi):
        start = bi * qb
        qblk = lax.dynamic_slice_in_dim(qp, start, qb, axis=1)
        cb = lax.dynamic_slice_in_dim(cq, start, qb, axis=1).transpose(0, 2, 1)
        tpos = q0 + start + jnp.arange(qb)
        s = (jnp.einsum('bqhd,bkhd->bhqk', qblk, k).astype(jnp.float32) * scale
             + (cb[..., None] - ct[:, :, None, :]))
        p = _masked_softmax(s, (kpos[None, :] <= tpos[:, None])[None, None])
        o = jnp.einsum('bhqk,bkhd->bqhd', p.astype(v.dtype), v)
        return o.reshape(B, qb, H * D)

    out = lax.map(block, jnp.arange(nb))
    return out.transpose(1, 0, 2, 3).reshape(B, tp, H * D)[:, :Tq]


def _layer(x, lp, past, q0):
    (n1, w1g, w1u, w1d, nm, w_in, b_f, nqn, nkn, pek, w1k, w2k, pev, w1v, w2v,
     fqn, fkn, wbn, wbf, wo, n2, w2g, w2u, w2d) = lp
    B, T, _ = x.shape
    pos = q0 + jnp.arange(T)
    x = x + 0.5 * _swiglu(_rmsnorm(x, n1), w1g, w1u, w1d)
    h = _rmsnorm(x, nm)
    z = h @ w_in
    split_idx = np.cumsum(np.array(SPLITS))[:-1].tolist()
    q_n, kv_n, g_n, qkv_f, f_n, g_m = jnp.split(z, split_idx, axis=-1)
    q_nsa = _rope(_rmsnorm(q_n.reshape(B, T, NSA_HEADS, HEAD_DIM), nqn), pos)
    kv_n = kv_n.reshape(B, T, 6, NSA_KV_HEADS, HEAD_DIM)
    slc_k = _rope(_rmsnorm(kv_n[:, :, 2], nkn[1]), pos)
    win_k = _rope(_rmsnorm(kv_n[:, :, 4], nkn[2]), pos)
    nsa_rows = jnp.stack([kv_n[:, :, 0], kv_n[:, :, 1], slc_k, kv_n[:, :, 3]], axis=2)
    win_rows = jnp.stack([win_k, kv_n[:, :, 5]], axis=2)
    g_nsa = jax.nn.sigmoid(g_n.reshape(B, T, NSA_HEADS, 3))
    qkv_f = qkv_f.reshape(B, T, 3, FOX_HEADS, HEAD_DIM)
    q_fox = _rmsnorm(qkv_f[:, :, 0], fqn)
    fox_rows = jnp.stack([_rmsnorm(qkv_f[:, :, 1], fkn), qkv_f[:, :, 2]], axis=2)
    logf_rows = jax.nn.log_sigmoid(f_n.astype(jnp.float32) + b_f.astype(jnp.float32))
    if past is None:
        nsa_all, fox_all, logf_all, win_all, win_k0 = nsa_rows, fox_rows, logf_rows, win_rows, q0
    else:
        p_nsa, p_fox, p_logf, p_win = past
        nsa_all = jnp.concatenate([p_nsa, nsa_rows.astype(p_nsa.dtype)], axis=1)
        fox_all = jnp.concatenate([p_fox, fox_rows.astype(p_fox.dtype)], axis=1)
        logf_all = jnp.concatenate([p_logf.astype(jnp.float32), logf_rows], axis=1)
        win_all = jnp.concatenate([p_win, win_rows.astype(p_win.dtype)], axis=1)
        win_k0 = q0 - p_win.shape[1]
    o_nsa = _nsa_attention(q_nsa, g_nsa.astype(q_nsa.dtype), nsa_all, win_all, q0, win_k0, nkn[0],
                           pek, w1k, w2k, pev, w1v, w2v)
    c = jnp.cumsum(logf_all.astype(jnp.float32), axis=1)
    o_fox = _fox_attention(q_fox, fox_all[:, :, 0], fox_all[:, :, 1], c, q0)
    gm = jax.nn.sigmoid(g_m)
    mixed = (gm[..., :D_MODEL] * (o_nsa @ wbn) + gm[..., D_MODEL:] * (o_fox @ wbf)) @ wo
    x = x + mixed
    x = x + 0.5 * _swiglu(_rmsnorm(x, n2), w2g, w2u, w2d)
    win_keep = min(WINDOW, win_all.shape[1])
    return x, nsa_rows, fox_rows, logf_rows, win_all[:, win_all.shape[1] - win_keep:]


def setup_inputs(seed: int = 0) -> dict:
    key = jax.random.key(seed)
    keys = jax.random.split(key, 32)
    f32 = jnp.float32
    n_pages = PAST_LEN // PAGE_SIZE
    n_used = DEC_BATCH * n_pages
    n_pool = n_used + max(1, n_used // 4)
    w_buf = min(WINDOW, PAST_LEN)
    nl = DEPTH

    def nrm(k, shape, scale=1.0):
        return scale * jax.random.normal(k, shape, f32)

    def gain(k, shape):
        return 1.0 + 0.1 * jax.random.normal(k, shape, f32)

    page_table = jax.random.permutation(keys[6], n_pool)[:n_used].reshape(DEC_BATCH, n_pages).astype(jnp.int32)
    return {
        'x_prompt': nrm(keys[0], (BATCH, SEQ, D_MODEL)),
        'x_sample': nrm(keys[1], (DEC_BATCH, DEC_SEQ, D_MODEL)),
        'cache_nsa_kv': nrm(keys[2], (nl, n_pool, PAGE_SIZE, 4, NSA_KV_HEADS, HEAD_DIM)),
        'cache_fox_kv': nrm(keys[3], (nl, n_pool, PAGE_SIZE, 2, FOX_HEADS, HEAD_DIM)),
        'cache_fox_logf': jax.nn.log_sigmoid(FORGET_BIAS + jax.random.normal(keys[4], (nl, n_pool, PAGE_SIZE, FOX_HEADS), f32)),
        'state_nsa_win_kv': nrm(keys[5], (nl, DEC_BATCH, w_buf, 2, NSA_KV_HEADS, HEAD_DIM)),
        'page_table': page_table,
        'norm_ffn1': gain(keys[7], (nl, D_MODEL)),
        'w1_gate': nrm(keys[8], (nl, D_MODEL, D_FF), D_MODEL ** -0.5),
        'w1_up': nrm(keys[9], (nl, D_MODEL, D_FF), D_MODEL ** -0.5),
        'w1_down': nrm(keys[10], (nl, D_FF, D_MODEL), D_FF ** -0.5),
        'norm_mix': gain(keys[11], (nl, D_MODEL)),
        'w_in': nrm(keys[12], (nl, D_MODEL, D_IN), D_MODEL ** -0.5),
        'b_forget': FORGET_BIAS + 0.5 * jax.random.normal(keys[13], (nl, FOX_HEADS), f32),
        'nsa_q_norm': gain(keys[14], (nl, HEAD_DIM)),
        'nsa_k_norm': gain(keys[15], (nl, 3, HEAD_DIM)),
        'cmp_pe_k': nrm(keys[16], (nl, CMP_BLOCK, HEAD_DIM), 0.1),
        'cmp_w1_k': nrm(keys[17], (nl, CMP_BLOCK, HEAD_DIM, CMP_HIDDEN), (CMP_BLOCK * HEAD_DIM) ** -0.5),
        'cmp_w2_k': nrm(keys[18], (nl, CMP_HIDDEN, HEAD_DIM), CMP_HIDDEN ** -0.5),
        'cmp_pe_v': nrm(keys[19], (nl, CMP_BLOCK, HEAD_DIM), 0.1),
        'cmp_w1_v': nrm(keys[20], (nl, CMP_BLOCK, HEAD_DIM, CMP_HIDDEN), (CMP_BLOCK * HEAD_DIM) ** -0.5),
        'cmp_w2_v': nrm(keys[21], (nl, CMP_HIDDEN, HEAD_DIM), CMP_HIDDEN ** -0.5),
        'fox_q_norm': gain(keys[22], (nl, HEAD_DIM)),
        'fox_k_norm': gain(keys[23], (nl, HEAD_DIM)),
        'w_branch_nsa': nrm(keys[24], (nl, NSA_Q_W, D_MODEL), NSA_Q_W ** -0.5),
        'w_branch_fox': nrm(keys[25], (nl, FOX_W, D_MODEL), FOX_W ** -0.5),
        'w_out': nrm(keys[26], (nl, D_MODEL, D_MODEL), D_MODEL ** -0.5),
        'norm_ffn2': gain(keys[27], (nl, D_MODEL)),
        'w2_gate': nrm(keys[28], (nl, D_MODEL, D_FF), D_MODEL ** -0.5),
        'w2_up': nrm(keys[29], (nl, D_MODEL, D_FF), D_MODEL ** -0.5),
        'w2_down': nrm(keys[30], (nl, D_FF, D_MODEL), D_FF ** -0.5),
    }


def reference(x_prompt, x_sample, cache_nsa_kv, cache_fox_kv, cache_fox_logf, state_nsa_win_kv, page_table,
              norm_ffn1, w1_gate, w1_up, w1_down, norm_mix, w_in, b_forget, nsa_q_norm, nsa_k_norm,
              cmp_pe_k, cmp_w1_k, cmp_w2_k, cmp_pe_v, cmp_w1_v, cmp_w2_v, fox_q_norm, fox_k_norm,
              w_branch_nsa, w_branch_fox, w_out, norm_ffn2, w2_gate, w2_up, w2_down):
    past_len = page_table.shape[1] * PAGE_SIZE
    y_prompt, y_sample = x_prompt, x_sample
    nsa_p, nsa_s, fox_p, fox_s, lf_p, lf_s, win_p, win_s = [], [], [], [], [], [], [], []
    for l in range(DEPTH):
        lp = (norm_ffn1[l], w1_gate[l], w1_up[l], w1_down[l], norm_mix[l], w_in[l], b_forget[l],
              nsa_q_norm[l], nsa_k_norm[l], cmp_pe_k[l], cmp_w1_k[l], cmp_w2_k[l], cmp_pe_v[l], cmp_w1_v[l],
              cmp_w2_v[l], fox_q_norm[l], fox_k_norm[l], w_branch_nsa[l], w_branch_fox[l], w_out[l],
              norm_ffn2[l], w2_gate[l], w2_up[l], w2_down[l])
        y_prompt, a_nsa, a_fox, a_lf, a_win = _layer(y_prompt, lp, None, 0)
        past = (_gather_pages(cache_nsa_kv, l, page_table), _gather_pages(cache_fox_kv, l, page_table),
                _gather_pages(cache_fox_logf, l, page_table), state_nsa_win_kv[l])
        y_sample, b_nsa, b_fox, b_lf, b_win = _layer(y_sample, lp, past, past_len)
        nsa_p.append(a_nsa)
        nsa_s.append(b_nsa)
        fox_p.append(a_fox)
        fox_s.append(b_fox)
        lf_p.append(a_lf)
        lf_s.append(b_lf)
        win_p.append(a_win)
        win_s.append(b_win)
    return (y_prompt, y_sample, jnp.stack(nsa_p), jnp.stack(nsa_s), jnp.stack(fox_p), jnp.stack(fox_s),
            jnp.stack(lf_p), jnp.stack(lf_s), jnp.stack(win_p), jnp.stack(win_s))
```

```python
import functools
import math

import numpy as np
import jax
import jax.numpy as jnp
from jax import lax
from jax.experimental import pallas as pl
from jax.experimental.pallas import tpu as pltpu

F32 = jnp.float32
BF = jnp.bfloat16

HEAD_DIM = 64
NSA_HEADS = 8
NSA_KV_HEADS = 2
FOX_HEADS = 8
ROT_DIM = 16
ROPE_THETA = 500000.0
CMP_BLOCK = 32
CMP_STRIDE = 16
CMP_HIDDEN = 128
SEL_BLOCK = 64
N_SEL = 16
WINDOW = 512
PAGE = 128
EPS = 1e-6
FORCE_SCORE = 1e6
SCALE = 0.125

LANES = 128
NQS = 8
NEG = -1e30
SEL_OFF = -float(2 ** 30)
VMEM_LIMIT = 56 * 1024 * 1024

NT = (((1,), (1,)), ((), ()))
TN = (((0,), (0,)), ((), ()))


def _cparams(*sem):
    return pltpu.CompilerParams(dimension_semantics=sem, vmem_limit_bytes=VMEM_LIMIT)


def _const_spec(shape):
    return pl.BlockSpec(shape, lambda *_: (0,) * len(shape), pipeline_mode=pl.Buffered(1))


def _split3(a):
    hi = a.astype(BF)
    r1 = a - hi.astype(F32)
    mid = r1.astype(BF)
    lo = (r1 - mid.astype(F32)).astype(BF)
    return hi, mid, lo


def _dot3(a, m):
    hi, mid, lo = _split3(a)
    return (jnp.dot(hi, m, preferred_element_type=F32) + jnp.dot(mid, m, preferred_element_type=F32)
            + jnp.dot(lo, m, preferred_element_type=F32))


def _rms_rows(x, gain):
    return x * lax.rsqrt(jnp.mean(x * x, axis=-1, keepdims=True) + EPS) * gain


def _ffn_body(x_ref, n_ref, wg_ref, wu_ref, wd_ref, o_ref):
    x = x_ref[...]
    h = _rms_rows(x, n_ref[...]).astype(BF)
    g = jnp.dot(h, wg_ref[...], preferred_element_type=F32)
    u = jnp.dot(h, wu_ref[...], preferred_element_type=F32)
    a = (g * jax.nn.sigmoid(g) * u).astype(BF)
    o_ref[...] = x + 0.5 * jnp.dot(a, wd_ref[...], preferred_element_type=F32)


def _ffn(x, n, wg, wu, wd, tm):
    rp, d = x.shape
    f = wg.shape[1]
    row = pl.BlockSpec((tm, d), lambda i: (i, 0))
    return pl.pallas_call(
        _ffn_body, grid=(rp // tm,),
        in_specs=[row, _const_spec((1, d)), _const_spec((d, f)), _const_spec((d, f)), _const_spec((f, d))],
        out_specs=row, out_shape=jax.ShapeDtypeStruct((rp, d), F32),
        compiler_params=_cparams("parallel"), name="ffn")(x, n, wg, wu, wd)


def _proj_body(x_ref, nm_ref, w_ref, seg_ref, gq_ref, gks_ref, gkw_ref, gfq_ref, gfk_ref, bm_ref, c_ref, s_ref,
               qn_ref, nsa_ref, win_ref, qf_ref, fox_ref, gm_ref, misc_ref, kaug_ref, vdup_ref, kwdup_ref, vwdup_ref,
               *, tm, d):
    x = x_ref[...]
    h = _rms_rows(x, nm_ref[...]).astype(BF)
    z = jnp.dot(h, w_ref[...], preferred_element_type=F32)
    cosv = c_ref[...]
    sinv = s_ref[...]
    seg = seg_ref[...]
    lane = lax.broadcasted_iota(jnp.int32, (tm, LANES), 1)
    first8 = (lane % HEAD_DIM) < (ROT_DIM // 2)
    low = lane < HEAD_DIM

    def hnorm(v, gain):
        zz = v * v
        hi = zz.astype(BF)
        lo = (zz - hi.astype(F32)).astype(BF)
        ms = jnp.dot(hi, seg, preferred_element_type=F32) + jnp.dot(lo, seg, preferred_element_type=F32)
        return v * lax.rsqrt(ms + EPS) * gain

    def rope(v):
        partner = jnp.where(first8, pltpu.roll(v, LANES - ROT_DIM // 2, 1), pltpu.roll(v, ROT_DIM // 2, 1))
        return v * cosv + partner * sinv

    def dup(v):
        r = pltpu.roll(v, HEAD_DIM, 1)
        return jnp.where(low, v, r), jnp.where(low, r, v)

    def chunk(off, c):
        return z[:, off + LANES * c: off + LANES * (c + 1)]

    o_q, o_kv, o_f, o_gm = 0, 512, 1280, 2816
    o_misc = o_gm + 2 * d
    for c in range(4):
        qn_ref[:, LANES * c: LANES * (c + 1)] = rope(hnorm(chunk(o_q, c), gq_ref[...]))
        qf_ref[:, LANES * c: LANES * (c + 1)] = hnorm(chunk(o_f, c), gfq_ref[...])
        fox_ref[:, LANES * c: LANES * (c + 1)] = hnorm(chunk(o_f + 512, c), gfk_ref[...])
    fox_ref[:, 512:1024] = z[:, o_f + 1024: o_f + 1536]
    nsa_ref[:, 0:256] = z[:, o_kv: o_kv + 256]
    ks = rope(hnorm(chunk(o_kv, 2), gks_ref[...]))
    vs = chunk(o_kv, 3)
    kw = rope(hnorm(chunk(o_kv, 4), gkw_ref[...]))
    vw = chunk(o_kv, 5)
    nsa_ref[:, 256:384] = ks
    nsa_ref[:, 384:512] = vs
    win_ref[:, 0:128] = kw
    win_ref[:, 128:256] = vw
    gm_ref[...] = jax.nn.sigmoid(z[:, o_gm: o_gm + 2 * d])
    zm = z[:, o_misc: o_misc + LANES] + bm_ref[...]
    logsig = jnp.minimum(zm, 0.0) - jnp.log1p(jnp.exp(-jnp.abs(zm)))
    misc_ref[...] = jnp.where(lane < 3 * NSA_HEADS, jax.nn.sigmoid(zm), logsig)
    pos = pl.program_id(0) * tm + lax.broadcasted_iota(jnp.int32, (tm, 1), 0)
    onehot = jnp.where((pos // SEL_BLOCK) % LANES == lane, 1.0, 0.0).astype(BF)
    k0, k1 = dup(ks * SCALE)
    v0, v1 = dup(vs)
    kw0, kw1 = dup(kw * SCALE)
    vw0, vw1 = dup(vw)
    for g, (kk, vv, kkw, vvw) in enumerate(((k0, v0, kw0, vw0), (k1, v1, kw1, vw1))):
        kaug_ref[g, :, 0:LANES] = onehot
        kaug_ref[g, :, LANES:2 * LANES] = kk.astype(BF)
        vdup_ref[g] = vv.astype(BF)
        kwdup_ref[g] = kkw.astype(BF)
        vwdup_ref[g] = vvw.astype(BF)


def _proj(x, nm, wa, seg, gq, gks, gkw, gfq, gfk, bm, cosv, sinv, tm):
    rp, d = x.shape
    n = wa.shape[1]
    row = lambda w: pl.BlockSpec((tm, w), lambda i: (i, 0))
    grp = lambda w: pl.BlockSpec((2, tm, w), lambda i: (0, i, 0))
    g128 = _const_spec((1, LANES))
    outs = [(512, F32), (512, F32), (256, F32), (512, F32), (1024, F32), (2 * d, F32), (LANES, F32)]
    out_shape = [jax.ShapeDtypeStruct((rp, w), dt) for w, dt in outs]
    out_shape += [jax.ShapeDtypeStruct((2, rp, w), BF) for w in (256, 128, 128, 128)]
    out_specs = [row(w) for w, _ in outs] + [grp(w) for w in (256, 128, 128, 128)]
    return pl.pallas_call(
        functools.partial(_proj_body, tm=tm, d=d), grid=(rp // tm,),
        in_specs=[row(d), _const_spec((1, d)), _const_spec((d, n)), _const_spec((LANES, LANES)),
                  g128, g128, g128, g128, g128, g128, row(LANES), row(LANES)],
        out_specs=out_specs, out_shape=out_shape,
        compiler_params=_cparams("parallel"), name="proj")(x, nm, wa, seg, gq, gks, gkw, gfq, gfk, bm, cosv, sinv)


def _merge_body(x_ref, oc_ref, os_ref, ow_ref, of_ref, misc_ref, gm_ref, e_ref, wbn_ref, wbf_ref, wo_ref, o_ref, *, d):
    hi, mid, lo = _split3(misc_ref[...])

    def gate(i):
        e = e_ref[i]
        return (jnp.dot(hi, e, preferred_element_type=F32) + jnp.dot(mid, e, preferred_element_type=F32)
                + jnp.dot(lo, e, preferred_element_type=F32))

    o_nsa = gate(0) * oc_ref[...] + gate(1) * os_ref[...] + gate(2) * ow_ref[...]
    a = jnp.dot(o_nsa.astype(BF), wbn_ref[...], preferred_element_type=F32)
    b = jnp.dot(of_ref[...].astype(BF), wbf_ref[...], preferred_element_type=F32)
    gm = gm_ref[...]
    mixed = gm[:, :d] * a + gm[:, d:] * b
    o_ref[...] = x_ref[...] + jnp.dot(mixed.astype(BF), wo_ref[...], preferred_element_type=F32)


def _merge(x, oc, os_, ow, of, misc, gm, e, wbn, wbf, wo, tm):
    rp, d = x.shape
    row = lambda w: pl.BlockSpec((tm, w), lambda i: (i, 0))
    return pl.pallas_call(
        functools.partial(_merge_body, d=d), grid=(rp // tm,),
        in_specs=[row(d), row(512), row(512), row(512), row(512), row(LANES), row(2 * d),
                  _const_spec((3, LANES, 512)), _const_spec((512, d)), _const_spec((512, d)), _const_spec((d, d))],
        out_specs=row(d), out_shape=jax.ShapeDtypeStruct((rp, d), F32),
        compiler_params=_cparams("parallel"), name="merge")(x, oc, os_, ow, of, misc, gm, e, wbn, wbf, wo)


def _cumsum_body(m_ref, u_ref, o_ref, carry_ref, *, tb):
    @pl.when(pl.program_id(0) == 0)
    def _():
        carry_ref[...] = jnp.zeros_like(carry_ref)

    lft = m_ref[...].T
    a = lft[3 * NSA_HEADS: 3 * NSA_HEADS + FOX_HEADS, :]
    c = _dot3(a, u_ref[...]) + carry_ref[:, 0:1]
    o_ref[...] = c
    carry_ref[...] = jnp.broadcast_to(c[:, tb - 1: tb], carry_ref.shape)


def _cumsum_t(misc, t, tb):
    u = jnp.asarray(np.triu(np.ones((tb, tb), np.float32)), BF)
    return pl.pallas_call(
        functools.partial(_cumsum_body, tb=tb), grid=(t // tb,),
        in_specs=[pl.BlockSpec((tb, LANES), lambda i: (i, 0)), _const_spec((tb, tb))],
        out_specs=pl.BlockSpec((FOX_HEADS, tb), lambda i: (0, i)),
        out_shape=jax.ShapeDtypeStruct((FOX_HEADS, t), F32),
        scratch_shapes=[pltpu.VMEM((FOX_HEADS, LANES), F32)],
        compiler_params=_cparams("arbitrary"), name="fox_cumsum")(misc, u)


def _tri_steps(n):
    qi = np.concatenate([np.full(i + 1, i, np.int32) for i in range(n)])
    kj = np.concatenate([np.arange(i + 1, dtype=np.int32) for i in range(n)])
    return jnp.asarray(qi), jnp.asarray(kj)


def _online_update(s, v, m_ref, l_ref, acc_ref, idx):
    m_prev = m_ref[idx]
    m_new = jnp.maximum(m_prev, jnp.max(s, axis=1, keepdims=True))
    alpha = jnp.exp(m_prev - m_new)
    p = jnp.exp(s - m_new)
    l_ref[idx] = alpha * l_ref[idx] + jnp.sum(p, axis=1, keepdims=True)
    acc_ref[idx] = alpha * acc_ref[idx] + jnp.dot(p.astype(BF), v, preferred_element_type=F32)
    m_ref[idx] = m_new


def _foxp_body(qi_ref, kj_ref, q_ref, k_ref, v_ref, c_ref, o_ref, qm_ref, m_ref, l_ref, acc_ref, *, tq):
    sid = pl.program_id(1)
    i = qi_ref[sid]
    j = kj_ref[sid]
    lane = lax.broadcasted_iota(jnp.int32, (tq, LANES), 1)
    low = lane < HEAD_DIM

    @pl.when(j == 0)
    def _():
        q = q_ref[...] * SCALE
        qm_ref[0] = jnp.where(low, q, 0.0).astype(BF)
        qm_ref[1] = jnp.where(low, 0.0, q).astype(BF)
        m_ref[...] = jnp.full_like(m_ref, NEG)
        l_ref[...] = jnp.zeros_like(l_ref)
        acc_ref[...] = jnp.zeros_like(acc_ref)

    def step(diag):
        k = k_ref[...].astype(BF)
        v = v_ref[...].astype(BF)
        for hh in range(2):
            s = lax.dot_general(qm_ref[hh], k, NT, preferred_element_type=F32) - c_ref[0, hh:hh + 1, :]
            if diag:
                r = lax.broadcasted_iota(jnp.int32, s.shape, 0)
                c = lax.broadcasted_iota(jnp.int32, s.shape, 1)
                s = jnp.where(c <= r, s, NEG)
            _online_update(s, v, m_ref, l_ref, acc_ref, hh)

    pl.when(j < i)(lambda: step(False))

    @pl.when(j == i)
    def _():
        step(True)
        o_ref[...] = jnp.where(low, acc_ref[0] / l_ref[0], acc_ref[1] / l_ref[1])


def _fox_prompt(qf, fox_rows, ct, t, tq):
    n = t // tq
    qi, kj = _tri_steps(n)
    npair = FOX_HEADS // 2
    gs = pltpu.PrefetchScalarGridSpec(
        num_scalar_prefetch=2, grid=(npair, qi.shape[0]),
        in_specs=[pl.BlockSpec((tq, LANES), lambda p, s, qi, kj: (qi[s], p)),
                  pl.BlockSpec((tq, LANES), lambda p, s, qi, kj: (kj[s], p)),
                  pl.BlockSpec((tq, LANES), lambda p, s, qi, kj: (kj[s], npair + p)),
                  pl.BlockSpec((1, 2, tq), lambda p, s, qi, kj: (p, 0, kj[s]))],
        out_specs=pl.BlockSpec((tq, LANES), lambda p, s, qi, kj: (qi[s], p)),
        scratch_shapes=[pltpu.VMEM((2, tq, LANES), BF), pltpu.VMEM((2, tq, 1), F32), pltpu.VMEM((2, tq, 1), F32),
                        pltpu.VMEM((2, tq, LANES), F32)])
    return pl.pallas_call(
        functools.partial(_foxp_body, tq=tq), grid_spec=gs,
        out_shape=jax.ShapeDtypeStruct((t, 512), F32),
        compiler_params=_cparams("parallel", "arbitrary"), name="fox_prompt")(qi, kj, qf, fox_rows, fox_rows, ct)


def _cmpab_body(pt_ref, *refs, pps):
    del pt_ref
    kp, vp = refs[:pps], refs[pps:2 * pps]
    wk_ref, wv_ref, abk_ref, abv_ref = refs[2 * pps:]
    nsub = PAGE // CMP_STRIDE
    acck = jnp.zeros(abk_ref.shape, F32)
    accv = jnp.zeros(abv_ref.shape, F32)
    for l in range(CMP_STRIDE):
        xk = jnp.concatenate([r[0, pl.ds(l, nsub, stride=CMP_STRIDE), :] for r in kp], axis=0).astype(BF)
        xv = jnp.concatenate([r[0, pl.ds(l, nsub, stride=CMP_STRIDE), :] for r in vp], axis=0).astype(BF)
        acck = acck + jnp.dot(xk, wk_ref[l], preferred_element_type=F32)
        accv = accv + jnp.dot(xv, wv_ref[l], preferred_element_type=F32)
    abk_ref[...] = acck
    abv_ref[...] = accv


def _cmp_ab(pool, pt_flat, wk, wv, pps):
    npg = pt_flat.shape[0]
    nsub = PAGE // CMP_STRIDE
    rows = pps * nsub
    kspec = [pl.BlockSpec((1, PAGE, LANES), lambda s, pt, k=k: (pt[s * pps + k], 0, 0)) for k in range(pps)]
    vspec = [pl.BlockSpec((1, PAGE, LANES), lambda s, pt, k=k: (pt[s * pps + k], 0, 1)) for k in range(pps)]
    wspec = pl.BlockSpec((CMP_STRIDE, LANES, 512), lambda s, pt: (0, 0, 0), pipeline_mode=pl.Buffered(1))
    ospec = pl.BlockSpec((rows, 512), lambda s, pt: (s, 0))
    gs = pltpu.PrefetchScalarGridSpec(num_scalar_prefetch=1, grid=(npg // pps,),
                                      in_specs=kspec + vspec + [wspec, wspec], out_specs=[ospec, ospec])
    shp = jax.ShapeDtypeStruct((npg * nsub, 512), F32)
    return pl.pallas_call(functools.partial(_cmpab_body, pps=pps), grid_spec=gs, out_shape=[shp, shp],
                          compiler_params=_cparams("parallel"), name="cmp_ab")(
        pt_flat, *([pool] * (2 * pps)), wk, wv)


def _gelu_tanh(x):
    return 0.5 * x * (1.0 + jnp.tanh(math.sqrt(2.0 / math.pi) * (x + 0.044715 * (x * x * x))))


def _cmpfin_body(abk_ref, abv_ref, pek_ref, w1k_ref, w2k_ref, pev_ref, w1v_ref, w2v_ref, seg_ref, gk_ref, c_ref, s_ref,
                 kc_ref, vc_ref, *, nc):
    lane = lax.broadcasted_iota(jnp.int32, (nc, LANES), 1)
    first8 = (lane % HEAD_DIM) < (ROT_DIM // 2)
    seg = seg_ref[...]
    cosv = c_ref[...]
    sinv = s_ref[...]

    def summaries(ab_ref, pe_ref, w1_ref, w2_ref, g):
        bias = jnp.dot(pe_ref[...].astype(BF), w1_ref[...].astype(BF), preferred_element_type=F32)[0:1, :]
        a = ab_ref[:, 256 * g: 256 * g + LANES]
        b = ab_ref[:, 256 * g + LANES: 256 * g + 2 * LANES]
        hid = _gelu_tanh(a + pltpu.roll(b, nc - 1, 0) + bias)
        return jnp.dot(hid.astype(BF), w2_ref[...], preferred_element_type=F32)

    for g in range(NSA_KV_HEADS):
        k = summaries(abk_ref, pek_ref, w1k_ref, w2k_ref, g)
        zz = k * k
        hi = zz.astype(BF)
        lo = (zz - hi.astype(F32)).astype(BF)
        ms = jnp.dot(hi, seg, preferred_element_type=F32) + jnp.dot(lo, seg, preferred_element_type=F32)
        k = k * lax.rsqrt(ms + EPS) * gk_ref[...]
        partner = jnp.where(first8, pltpu.roll(k, LANES - ROT_DIM // 2, 1), pltpu.roll(k, ROT_DIM // 2, 1))
        k = k * cosv + partner * sinv
        kc_ref[0, :, LANES * g: LANES * (g + 1)] = (k * SCALE).astype(BF)
        vc_ref[0, :, LANES * g: LANES * (g + 1)] = summaries(abv_ref, pev_ref, w1v_ref, w2v_ref, g).astype(BF)


def _cmp_fin(abk, abv, pek, w1k, w2k, pev, w1v, w2v, seg, gk, cosc, sinc, nseq, nc):
    ab = pl.BlockSpec((nc, 512), lambda b: (b, 0))
    kdim = CMP_BLOCK * HEAD_DIM
    out = pl.BlockSpec((1, nc, 256), lambda b: (b, 0, 0))
    shp = jax.ShapeDtypeStruct((nseq, nc, 256), BF)
    return pl.pallas_call(
        functools.partial(_cmpfin_body, nc=nc), grid=(nseq,),
        in_specs=[ab, ab, _const_spec((8, kdim)), _const_spec((kdim, CMP_HIDDEN)), _const_spec((CMP_HIDDEN, LANES)),
                  _const_spec((8, kdim)), _const_spec((kdim, CMP_HIDDEN)), _const_spec((CMP_HIDDEN, LANES)),
                  _const_spec((LANES, LANES)), _const_spec((1, LANES)), _const_spec((nc, LANES)), _const_spec((nc, LANES))],
        out_specs=[out, out], out_shape=[shp, shp],
        compiler_params=_cparams("parallel"), name="cmp_fin")(abk, abv, pek, w1k, w2k, pev, w1v, w2v, seg, gk, cosc, sinc)


def _cmpattn_body(seq_ref, pos_ref, q_ref, kc_ref, vc_ref, m_ref, oc_ref, sb_ref, *, nq, nc, nl, nblk, k_pick):
    del seq_ref
    pos0 = pos_ref[pl.program_id(0)]
    q = q_ref[0]
    lane = lax.broadcasted_iota(jnp.int32, (nq, LANES), 1)
    low = lane < HEAD_DIM
    trow = pos0 + lax.broadcasted_iota(jnp.int32, (nq, 1), 0)
    t4 = jnp.concatenate([trow] * 4, axis=0)
    posc = CMP_STRIDE * lax.broadcasted_iota(jnp.int32, (1, nc), 1) + (CMP_BLOCK - 1)
    blk = lax.broadcasted_iota(jnp.int32, (1, nl), 1)
    blkf = blk.astype(F32)
    valid = posc <= t4
    mm = m_ref[...]
    for g in range(NSA_KV_HEADS):
        rows = []
        for hh in range(4):
            h = 4 * g + hh
            pr = q[:, LANES * (h // 2): LANES * (h // 2 + 1)]
            rows.append(jnp.where(low, pr, 0.0) if h % 2 == 0 else jnp.where(low, 0.0, pr))
        qg = jnp.concatenate(rows, axis=0).astype(BF)
        s = lax.dot_general(qg, kc_ref[0, :, LANES * g: LANES * (g + 1)], NT, preferred_element_type=F32)
        s = jnp.where(valid, s, NEG)
        m = jnp.max(s, axis=1, keepdims=True)
        e = jnp.where(valid, jnp.exp(s - m), 0.0)
        dsum = jnp.sum(e, axis=1, keepdims=True)
        p = e / jnp.where(dsum > 0, dsum, 1.0)
        o = jnp.dot(p.astype(BF), vc_ref[0, :, LANES * g: LANES * (g + 1)], preferred_element_type=F32)
        for pj in range(2):
            oc_ref[0, :, 256 * g + LANES * pj: 256 * g + LANES * (pj + 1)] = jnp.where(
                low, o[2 * pj * nq: (2 * pj + 1) * nq], o[(2 * pj + 1) * nq: (2 * pj + 2) * nq])
        imp = p[0:nq] + p[nq:2 * nq] + p[2 * nq:3 * nq] + p[3 * nq:4 * nq]
        ps = _dot3(imp, mm)
        cur = trow // SEL_BLOCK
        force = (blk == 0) | (blk == cur) | (blk == cur - 1)
        score = jnp.where(force, FORCE_SCORE, jnp.where(blk * SEL_BLOCK <= trow, ps, -1.0))
        score = jnp.where(blk < nblk, score, -1.0)

        def pick(_, carry):
            sc, sel = carry
            mx = jnp.max(sc, axis=1, keepdims=True)
            idx = jnp.min(jnp.where(sc == mx, blkf, 1e9), axis=1, keepdims=True)
            hit = blkf == idx
            return jnp.where(hit, -2.0, sc), jnp.where(hit, 1.0, sel)

        _, sel = lax.fori_loop(0, k_pick, pick, (score, jnp.zeros((nq, nl), F32)))
        sb_ref[0, :, nl * g: nl * (g + 1)] = jnp.where(sel > 0, 0.0, SEL_OFF).astype(BF)


def _cmp_attn(q3, kcd, vcd, mmat, seq_of, pos0, nq, nblk, k_pick):
    nb = q3.shape[0]
    nc = kcd.shape[1]
    nl = mmat.shape[1]
    gs = pltpu.PrefetchScalarGridSpec(
        num_scalar_prefetch=2, grid=(nb,),
        in_specs=[pl.BlockSpec((1, nq, 512), lambda b, sq, ps: (b, 0, 0)),
                  pl.BlockSpec((1, nc, 256), lambda b, sq, ps: (sq[b], 0, 0)),
                  pl.BlockSpec((1, nc, 256), lambda b, sq, ps: (sq[b], 0, 0)),
                  pl.BlockSpec((nc, nl), lambda b, sq, ps: (0, 0), pipeline_mode=pl.Buffered(1))],
        out_specs=[pl.BlockSpec((1, nq, 512), lambda b, sq, ps: (b, 0, 0)),
                   pl.BlockSpec((1, nq, 2 * nl), lambda b, sq, ps: (b, 0, 0))])
    return pl.pallas_call(
        functools.partial(_cmpattn_body, nq=nq, nc=nc, nl=nl, nblk=nblk, k_pick=k_pick), grid_spec=gs,
        out_shape=[jax.ShapeDtypeStruct((nb, nq, 512), F32), jax.ShapeDtypeStruct((nb, nq, 2 * nl), BF)],
        compiler_params=_cparams("parallel"), name="cmp_attn")(seq_of, pos0, q3, kcd, vcd, mmat)


def _group_rows(q, low, width_tail=None):
    rows = []
    for hh in range(4):
        pr = q[:, LANES * (hh // 2): LANES * (hh // 2 + 1)]
        rows.append(jnp.where(low, pr, 0.0) if hh % 2 == 0 else jnp.where(low, 0.0, pr))
    return rows


def _store_group(o_ref, o, low, n):
    for pj in range(2):
        o_ref[:, LANES * pj: LANES * (pj + 1)] = jnp.where(low, o[2 * pj * n: (2 * pj + 1) * n],
                                                           o[(2 * pj + 1) * n: (2 * pj + 2) * n])


def _selp_body(qi_ref, kj_ref, q_ref, sb_ref, ka_ref, v_ref, o_ref, m_ref, l_ref, acc_ref, *, tq):
    sid = pl.program_id(1)
    i = qi_ref[sid]
    j = kj_ref[sid]
    low = lax.broadcasted_iota(jnp.int32, (tq, LANES), 1) < HEAD_DIM

    @pl.when(j == 0)
    def _():
        m_ref[...] = jnp.full_like(m_ref, NEG)
        l_ref[...] = jnp.zeros_like(l_ref)
        acc_ref[...] = jnp.zeros_like(acc_ref)

    def step(diag):
        sb = sb_ref[...]
        qa = jnp.concatenate([jnp.concatenate([sb, r.astype(BF)], axis=1) for r in _group_rows(q_ref[...], low)], axis=0)
        s = lax.dot_general(qa, ka_ref[0], NT, preferred_element_type=F32)
        if diag:
            r = lax.broadcasted_iota(jnp.int32, s.shape, 0) % tq
            c = lax.broadcasted_iota(jnp.int32, s.shape, 1)
            s = jnp.where(c <= r, s, NEG)
        _online_update(s, v_ref[0], m_ref, l_ref, acc_ref, 0)

    pl.when(j < i)(lambda: step(False))

    @pl.when(j == i)
    def _():
        step(True)
        _store_group(o_ref, acc_ref[0] / l_ref[0], low, tq)


def _sel_prompt(qn, selb, kaug, vdup, t, tq, nl):
    n = t // tq
    qi, kj = _tri_steps(n)
    nsup = nl // LANES
    per_sup = LANES * SEL_BLOCK // tq
    gs = pltpu.PrefetchScalarGridSpec(
        num_scalar_prefetch=2, grid=(NSA_KV_HEADS, qi.shape[0]),
        in_specs=[pl.BlockSpec((tq, 256), lambda g, s, qi, kj: (qi[s], g)),
                  pl.BlockSpec((tq, LANES), lambda g, s, qi, kj: (qi[s], g * nsup + kj[s] // per_sup)),
                  pl.BlockSpec((1, tq, 256), lambda g, s, qi, kj: (g, kj[s], 0)),
                  pl.BlockSpec((1, tq, LANES), lambda g, s, qi, kj: (g, kj[s], 0))],
        out_specs=pl.BlockSpec((tq, 256), lambda g, s, qi, kj: (qi[s], g)),
        scratch_shapes=[pltpu.VMEM((1, 4 * tq, 1), F32), pltpu.VMEM((1, 4 * tq, 1), F32),
                        pltpu.VMEM((1, 4 * tq, LANES), F32)])
    return pl.pallas_call(
        functools.partial(_selp_body, tq=tq), grid_spec=gs,
        out_shape=jax.ShapeDtypeStruct((t, 512), F32),
        compiler_params=_cparams("parallel", "arbitrary"), name="sel_prompt")(qi, kj, qn, selb, kaug, vdup)


def _winp_body(q_ref, *refs, tq, nprev):
    nt = nprev + 1
    k_refs, v_refs, o_ref = refs[:nt], refs[nt:2 * nt], refs[2 * nt]
    i = pl.program_id(1)
    low = lax.broadcasted_iota(jnp.int32, (tq, LANES), 1) < HEAD_DIM
    qg = jnp.concatenate(_group_rows(q_ref[...], low), axis=0).astype(BF)
    k = jnp.concatenate([r[0] for r in k_refs], axis=0)
    v = jnp.concatenate([r[0] for r in v_refs], axis=0)
    s = lax.dot_general(qg, k, NT, preferred_element_type=F32)
    tpos = i * tq + lax.broadcasted_iota(jnp.int32, s.shape, 0) % tq
    kpos = (i - nprev) * tq + lax.broadcasted_iota(jnp.int32, s.shape, 1)
    ok = (kpos <= tpos) & (tpos - kpos < WINDOW) & (kpos >= 0)
    s = jnp.where(ok, s, NEG)
    m = jnp.max(s, axis=1, keepdims=True)
    e = jnp.exp(s - m)
    o = jnp.dot(e.astype(BF), v, preferred_element_type=F32) / jnp.sum(e, axis=1, keepdims=True)
    _store_group(o_ref, o, low, tq)


def _win_prompt(qn, kwdup, vwdup, t, tq):
    nprev = WINDOW // tq
    kv = [pl.BlockSpec((1, tq, LANES), lambda g, i, dd=dd: (g, jnp.maximum(i - nprev + dd, 0), 0)) for dd in range(nprev + 1)]
    return pl.pallas_call(
        functools.partial(_winp_body, tq=tq, nprev=nprev), grid=(NSA_KV_HEADS, t // tq),
        in_specs=[pl.BlockSpec((tq, 256), lambda g, i: (i, g))] + kv + kv,
        out_specs=pl.BlockSpec((tq, 256), lambda g, i: (i, g)),
        out_shape=jax.ShapeDtypeStruct((t, 512), F32),
        compiler_params=_cparams("parallel", "parallel"), name="win_prompt")(
        qn, *([kwdup] * (nprev + 1)), *([vwdup] * (nprev + 1)))


def _sample_rows(q):
    low = lax.broadcasted_iota(jnp.int32, (NQS, LANES), 1) < HEAD_DIM
    rows = []
    for h in range(NSA_HEADS):
        pr = q[:, LANES * (h // 2): LANES * (h // 2 + 1)]
        if (h % 2) != (h // 4):
            pr = pltpu.roll(pr, HEAD_DIM, 1)
        rows.append(jnp.where(low, pr, 0.0) if h // 4 == 0 else jnp.where(low, 0.0, pr))
    return jnp.concatenate(rows, axis=0) * SCALE


def _sample_store(o_ref, o):
    low = lax.broadcasted_iota(jnp.int32, (NQS, LANES), 1) < HEAD_DIM
    for pj in range(4):
        a = o[2 * pj * NQS: (2 * pj + 1) * NQS]
        b = o[(2 * pj + 1) * NQS: (2 * pj + 2) * NQS]
        if pj // 2 == 0:
            b = pltpu.roll(b, HEAD_DIM, 1)
        else:
            a = pltpu.roll(a, HEAD_DIM, 1)
        o_ref[0, :, LANES * pj: LANES * (pj + 1)] = jnp.where(low, a, b)


def _new_key_mask(shape, row_to_query):
    r = row_to_query(lax.broadcasted_iota(jnp.int32, shape, 0))
    c = lax.broadcasted_iota(jnp.int32, shape, 1)
    return c <= r


def _sels_body(pt_ref, q_ref, sb0_ref, sb1_ref, e_ref, kn_ref, vn_ref, *refs, pps, nsteps, per_chunk):
    del pt_ref
    pages = refs[:pps]
    o_ref, qs_ref, m_ref, l_ref, acc_ref = refs[pps:]
    sid = pl.program_id(1)

    @pl.when(sid == 0)
    def _():
        qs_ref[...] = _sample_rows(q_ref[0])
        m_ref[...] = jnp.full_like(m_ref, NEG)
        l_ref[...] = jnp.zeros_like(l_ref)
        acc_ref[...] = jnp.zeros_like(acc_ref)

    qs = qs_ref[...]
    k = jnp.concatenate([r[0, :, 0:LANES] for r in pages], axis=0).astype(BF)
    v = jnp.concatenate([r[0, :, LANES:2 * LANES] for r in pages], axis=0).astype(BF)
    sbrows = jnp.concatenate([sb0_ref[0].astype(F32)] * 4 + [sb1_ref[0].astype(F32)] * 4, axis=0).astype(BF)
    bias = jnp.dot(sbrows, e_ref[sid % per_chunk], preferred_element_type=F32)
    s = lax.dot_general(qs.astype(BF), k, NT, preferred_element_type=F32) + bias
    _online_update(s, v, m_ref, l_ref, acc_ref, 0)

    @pl.when(sid == nsteps - 1)
    def _():
        kn = kn_ref[0][:, 256:384]
        vn = vn_ref[0][:, 384:512]
        sn = lax.dot_general(qs, kn, NT, preferred_element_type=F32)
        sn = jnp.where(_new_key_mask(sn.shape, lambda r: r % NQS), sn, NEG)
        m_prev = m_ref[0]
        m_new = jnp.maximum(m_prev, jnp.max(sn, axis=1, keepdims=True))
        alpha = jnp.exp(m_prev - m_new)
        pn = jnp.exp(sn - m_new)
        l = alpha * l_ref[0] + jnp.sum(pn, axis=1, keepdims=True)
        acc = alpha * acc_ref[0] + jnp.dot(pn, vn, preferred_element_type=F32)
        _sample_store(o_ref, acc / l)


def _sel_sample(q3, selb, nsa_new3, pool, pt_flat, eexp, b, npg, pps, nl):
    nsteps = npg // pps
    per_chunk = eexp.shape[0]
    nchunk = nl // LANES
    pg = [pl.BlockSpec((1, PAGE, 256), lambda bb, s, pt, k=k: (pt[bb * npg + s * pps + k], 0, 1)) for k in range(pps)]
    row = pl.BlockSpec((1, NQS, 512), lambda bb, s, pt: (bb, 0, 0))
    gs = pltpu.PrefetchScalarGridSpec(
        num_scalar_prefetch=1, grid=(b, nsteps),
        in_specs=[row,
                  pl.BlockSpec((1, NQS, LANES), lambda bb, s, pt: (bb, 0, s // per_chunk)),
                  pl.BlockSpec((1, NQS, LANES), lambda bb, s, pt: (bb, 0, nchunk + s // per_chunk)),
                  pl.BlockSpec(eexp.shape, lambda bb, s, pt: (0, 0, 0), pipeline_mode=pl.Buffered(1)),
                  row, row] + pg,
        out_specs=row,
        scratch_shapes=[pltpu.VMEM((8 * NQS, LANES), F32), pltpu.VMEM((1, 8 * NQS, 1), F32),
                        pltpu.VMEM((1, 8 * NQS, 1), F32), pltpu.VMEM((1, 8 * NQS, LANES), F32)])
    return pl.pallas_call(
        functools.partial(_sels_body, pps=pps, nsteps=nsteps, per_chunk=per_chunk), grid_spec=gs,
        out_shape=jax.ShapeDtypeStruct((b, NQS, 512), F32),
        compiler_params=_cparams("parallel", "arbitrary"), name="sel_sample")(
        pt_flat, q3, selb, selb, eexp, nsa_new3, nsa_new3, *([pool] * pps))


def _wins_body(q_ref, st_ref, wn_ref, o_ref, *, wb):
    qs = _sample_rows(q_ref[0])
    st = st_ref[0]
    k = st[:, 0:LANES].astype(BF)
    v = st[:, LANES:2 * LANES].astype(BF)
    s = lax.dot_general(qs.astype(BF), k, NT, preferred_element_type=F32)
    tq = lax.broadcasted_iota(jnp.int32, s.shape, 0) % NQS
    n = lax.broadcasted_iota(jnp.int32, s.shape, 1)
    s = jnp.where(n > tq + (wb - WINDOW), s, NEG)
    wn = wn_ref[0]
    sn = lax.dot_general(qs, wn[:, 0:LANES], NT, preferred_element_type=F32)
    sn = jnp.where(_new_key_mask(sn.shape, lambda r: r % NQS), sn, NEG)
    m = jnp.maximum(jnp.max(s, axis=1, keepdims=True), jnp.max(sn, axis=1, keepdims=True))
    e = jnp.exp(s - m)
    en = jnp.exp(sn - m)
    den = jnp.sum(e, axis=1, keepdims=True) + jnp.sum(en, axis=1, keepdims=True)
    o = jnp.dot(e.astype(BF), v, preferred_element_type=F32) + jnp.dot(en, wn[:, LANES:2 * LANES], preferred_element_type=F32)
    _sample_store(o_ref, o / den)


def _win_sample(q3, state, win_new3, b, wb):
    row = pl.BlockSpec((1, NQS, 512), lambda bb: (bb, 0, 0))
    return pl.pallas_call(
        functools.partial(_wins_body, wb=wb), grid=(b,),
        in_specs=[row, pl.BlockSpec((1, wb, 256), lambda bb: (bb, 0, 0)), pl.BlockSpec((1, NQS, 256), lambda bb: (bb, 0, 0))],
        out_specs=row, out_shape=jax.ShapeDtypeStruct((b, NQS, 512), F32),
        compiler_params=_cparams("parallel"), name="win_sample")(q3, state, win_new3)


def _foxs_body(pt_ref, q_ref, fn_ref, mn_ref, u_ref, *refs, pps, nsteps):
    del pt_ref
    kv = refs[:pps]
    lfp = refs[pps:2 * pps]
    o_ref, qs_ref, m_ref, l_ref, acc_ref, carry_ref = refs[2 * pps:]
    sid = pl.program_id(1)
    sub = lax.broadcasted_iota(jnp.int32, (NQS, 512), 0)
    lane = lax.broadcasted_iota(jnp.int32, (NQS, 512), 1)
    headmask = (lane // HEAD_DIM) == sub

    @pl.when(sid == 0)
    def _():
        q = q_ref[0] * SCALE
        qs_ref[...] = jnp.concatenate(
            [jnp.where(headmask, jnp.broadcast_to(q[t:t + 1, :], (NQS, 512)), 0.0) for t in range(NQS)], axis=0)
        m_ref[...] = jnp.full_like(m_ref, NEG)
        l_ref[...] = jnp.zeros_like(l_ref)
        acc_ref[...] = jnp.zeros_like(acc_ref)
        carry_ref[...] = jnp.zeros_like(carry_ref)

    qs = qs_ref[...]
    k = jnp.concatenate([r[0, :, 0:512] for r in kv], axis=0).astype(BF)
    v = jnp.concatenate([r[0, :, 512:1024] for r in kv], axis=0).astype(BF)
    lf = jnp.concatenate([r[0] for r in lfp], axis=0)
    hi, mid, lo = _split3(lf)
    u = u_ref[...]
    cum = (lax.dot_general(hi, u, TN, preferred_element_type=F32) + lax.dot_general(mid, u, TN, preferred_element_type=F32)
           + lax.dot_general(lo, u, TN, preferred_element_type=F32)) + carry_ref[:, 0:1]
    carry_ref[...] = jnp.broadcast_to(cum[:, cum.shape[1] - 1:], carry_ref.shape)
    s = lax.dot_general(qs.astype(BF), k, NT, preferred_element_type=F32) - jnp.concatenate([cum] * NQS, axis=0)
    _online_update(s, v, m_ref, l_ref, acc_ref, 0)

    @pl.when(sid == nsteps - 1)
    def _():
        fn = fn_ref[0]
        kn = fn[:, 0:512]
        vn = fn[:, 512:1024]
        lfn = jnp.concatenate([mn_ref[0], jnp.zeros((LANES - NQS, LANES), F32)], axis=0).T
        lfn = lfn[3 * NSA_HEADS: 3 * NSA_HEADS + FOX_HEADS, 0:NQS]
        col = lax.broadcasted_iota(jnp.int32, (FOX_HEADS, NQS), 1)
        cn = carry_ref[:, 0:NQS]
        for i in range(NQS):
            cn = cn + jnp.where(col >= i, lfn[:, i:i + 1], 0.0)
        sn = lax.dot_general(qs, kn, NT, preferred_element_type=F32) - jnp.concatenate([cn] * NQS, axis=0)
        sn = jnp.where(_new_key_mask(sn.shape, lambda r: r // NQS), sn, NEG)
        m_prev = m_ref[0]
        m_new = jnp.maximum(m_prev, jnp.max(sn, axis=1, keepdims=True))
        alpha = jnp.exp(m_prev - m_new)
        pn = jnp.exp(sn - m_new)
        l = alpha * l_ref[0] + jnp.sum(pn, axis=1, keepdims=True)
        acc = (alpha * acc_ref[0] + jnp.dot(pn, vn, preferred_element_type=F32)) / l
        o_ref[0] = jnp.concatenate(
            [jnp.sum(jnp.where(headmask, acc[NQS * t: NQS * (t + 1)], 0.0), axis=0, keepdims=True) for t in range(NQS)],
            axis=0)


def _fox_sample(qf3, fox_new3, misc_new3, pool_kv, pool_lf, pt_flat, b, npg, pps):
    nsteps = npg // pps
    keys = pps * PAGE
    u = jnp.asarray(np.triu(np.ones((keys, keys), np.float32)), BF)
    kvs = [pl.BlockSpec((1, PAGE, 1024), lambda bb, s, pt, k=k: (pt[bb * npg + s * pps + k], 0, 0)) for k in range(pps)]
    lfs = [pl.BlockSpec((1, PAGE, FOX_HEADS), lambda bb, s, pt, k=k: (pt[bb * npg + s * pps + k], 0, 0)) for k in range(pps)]
    row = lambda w: pl.BlockSpec((1, NQS, w), lambda bb, s, pt: (bb, 0, 0))
    gs = pltpu.PrefetchScalarGridSpec(
        num_scalar_prefetch=1, grid=(b, nsteps),
        in_specs=[row(512), row(1024), row(LANES),
                  pl.BlockSpec((keys, keys), lambda bb, s, pt: (0, 0), pipeline_mode=pl.Buffered(1))] + kvs + lfs,
        out_specs=row(512),
        scratch_shapes=[pltpu.VMEM((8 * NQS, 512), F32), pltpu.VMEM((1, 8 * NQS, 1), F32), pltpu.VMEM((1, 8 * NQS, 1), F32),
                        pltpu.VMEM((1, 8 * NQS, 512), F32), pltpu.VMEM((FOX_HEADS, LANES), F32)])
    return pl.pallas_call(
        functools.partial(_foxs_body, pps=pps, nsteps=nsteps), grid_spec=gs,
        out_shape=jax.ShapeDtypeStruct((b, NQS, 512), F32),
        compiler_params=_cparams("parallel", "arbitrary"), name="fox_sample")(
        pt_flat, qf3, fox_new3, misc_new3, u, *([pool_kv] * pps), *([pool_lf] * pps))


def _rope_tables(pos):
    half = ROT_DIM // 2
    inv = ROPE_THETA ** (-(jnp.arange(half, dtype=F32) * 2.0 / ROT_DIM))
    ang = pos.astype(F32)[:, None] * inv[None, :]
    cos, sin = jnp.cos(ang), jnp.sin(ang)
    n = pos.shape[0]
    c64 = jnp.concatenate([cos, cos, jnp.ones((n, HEAD_DIM - ROT_DIM), F32)], axis=1)
    s64 = jnp.concatenate([-sin, sin, jnp.zeros((n, HEAD_DIM - ROT_DIM), F32)], axis=1)
    return jnp.tile(c64, (1, 2)), jnp.tile(s64, (1, 2))


def _importance_matrix(nc, nl):
    m = np.zeros((nc, nl), np.float32)
    per = SEL_BLOCK // CMP_STRIDE
    for n in range(nc - 1):
        m[n, n // per] += 1.0
        m[n, (n + 1) // per] += 1.0
    return jnp.asarray(m, BF)


def _gate_expand():
    e = np.zeros((3, LANES, 512), np.float32)
    for i in range(3):
        for h in range(NSA_HEADS):
            e[i, 3 * h + i, HEAD_DIM * h: HEAD_DIM * (h + 1)] = 1.0
    return jnp.asarray(e, BF)


def _block_expand(pps):
    per_step = pps * PAGE // SEL_BLOCK
    per_chunk = LANES // per_step
    e = np.zeros((per_chunk, LANES, pps * PAGE), np.float32)
    for r in range(per_chunk):
        for key in range(pps * PAGE):
            e[r, r * per_step + key // SEL_BLOCK, key] = 1.0
    return jnp.asarray(e, BF)


def _cmp_weights(w1):
    top, bot = w1[:CMP_STRIDE], w1[CMP_STRIDE:]
    wl = jnp.concatenate([top, bot], axis=2)
    z = jnp.zeros_like(wl)
    return jnp.concatenate([jnp.concatenate([wl, z], axis=2), jnp.concatenate([z, wl], axis=2)], axis=1).astype(BF)


def _pad_rows(a, rp):
    return jnp.pad(a, ((0, rp - a.shape[0]),) + ((0, 0),) * (a.ndim - 1))


def kernel(x_prompt, x_sample, cache_nsa_kv, cache_fox_kv, cache_fox_logf, state_nsa_win_kv, page_table, norm_ffn1, w1_gate, w1_up, w1_down, norm_mix, w_in, b_forget, nsa_q_norm, nsa_k_norm, cmp_pe_k, cmp_w1_k, cmp_w2_k, cmp_pe_v, cmp_w1_v, cmp_w2_v, fox_q_norm, fox_k_norm, w_branch_nsa, w_branch_fox, w_out, norm_ffn2, w2_gate, w2_up, w2_down):
    bp, t, d = x_prompt.shape
    b, ts, _ = x_sample.shape
    assert bp == 1 and norm_ffn1.shape[0] == 1 and ts <= 4
    npg = page_table.shape[1]
    past = npg * PAGE
    n_pool = cache_nsa_kv.shape[1]
    wb = state_nsa_win_kv.shape[2]
    tm = 256
    tq_fox = min(512, t)
    tq_sel = min(512, t)
    tq_win = 256
    nq_cmp = 128
    pps_cmp, pps_sel, pps_fox = 16, 8, 8
    assert t % PAGE == 0 and t % tq_fox == 0 and t % tq_win == 0 and WINDOW % tq_win == 0
    assert (t // PAGE) % pps_cmp == 0 and npg % pps_cmp == 0 and npg % pps_sel == 0 and npg % pps_fox == 0

    xs = jnp.pad(x_sample, ((0, 0), (0, NQS - ts), (0, 0))).reshape(b * NQS, d)
    r = t + b * NQS
    rp = -(-r // tm) * tm
    x_all = _pad_rows(jnp.concatenate([x_prompt[0], xs], axis=0), rp)
    pos = jnp.concatenate([jnp.arange(t), past + (jnp.arange(b * NQS) % NQS), jnp.zeros((rp - r,), jnp.int32)])
    cosv, sinv = _rope_tables(pos)

    f = w1_gate.shape[2]
    fp = -(-f // LANES) * LANES
    padc = lambda w: jnp.pad(w.astype(BF), ((0, 0), (0, fp - f)))
    padr = lambda w: jnp.pad(w.astype(BF), ((0, fp - f), (0, 0)))
    wi = w_in[0]
    o2, o3, o4, o5 = 1280, 1304, 2840, 2848
    wa = jnp.concatenate([wi[:, :o2], wi[:, o3:o4], wi[:, o5:], wi[:, o2:o3], wi[:, o4:o5],
                          jnp.zeros((d, LANES - 32), F32)], axis=1).astype(BF)
    bm = jnp.zeros((1, LANES), F32).at[0, 24:32].set(b_forget[0])
    g2 = lambda g: jnp.tile(g, 2)[None, :]
    seg = jnp.asarray(np.kron(np.eye(2, dtype=np.float32), np.full((HEAD_DIM, HEAD_DIM), 1.0 / HEAD_DIM, np.float32)), BF)

    x1 = _ffn(x_all, norm_ffn1, padc(w1_gate[0]), padc(w1_up[0]), padr(w1_down[0]), tm)
    (qn, nsa_rows, win_rows, qf, fox_rows, gm, misc, kaug, vdup, kwdup, vwdup) = _proj(
        x1, norm_mix, wa, seg, g2(nsa_q_norm[0]), g2(nsa_k_norm[0, 1]), g2(nsa_k_norm[0, 2]), g2(fox_q_norm[0]),
        g2(fox_k_norm[0]), bm, cosv, sinv, tm)

    ct = _cumsum_t(misc, t, min(512, t)).reshape(FOX_HEADS // 2, 2, t)
    o_fox_p = _fox_prompt(qf, fox_rows, ct, t, tq_fox)

    wk, wv = _cmp_weights(cmp_w1_k[0]), _cmp_weights(cmp_w1_v[0])
    kdim = CMP_BLOCK * HEAD_DIM
    pe8 = lambda pe: jnp.broadcast_to(pe.reshape(1, kdim), (8, kdim))
    w2d = lambda w2: jnp.concatenate([w2, w2], axis=1).astype(BF)
    gkc = g2(nsa_k_norm[0, 0])

    def compress(pool, pt_flat, nseq, nc):
        abk, abv = _cmp_ab(pool, pt_flat, wk, wv, pps_cmp)
        posc = jnp.arange(nc) * CMP_STRIDE + (CMP_BLOCK - 1)
        cosc, sinc = _rope_tables(posc)
        return _cmp_fin(abk, abv, pe8(cmp_pe_k[0]), cmp_w1_k[0].reshape(kdim, CMP_HIDDEN), w2d(cmp_w2_k[0]),
                        pe8(cmp_pe_v[0]), cmp_w1_v[0].reshape(kdim, CMP_HIDDEN), w2d(cmp_w2_v[0]), seg, gkc, cosc, sinc,
                        nseq, nc)

    nsub = PAGE // CMP_STRIDE
    tpg = t // PAGE
    kc_p, vc_p = compress(nsa_rows[:t].reshape(tpg, PAGE, 512), jnp.arange(tpg, dtype=jnp.int32), 1, tpg * nsub)
    pool_nsa = cache_nsa_kv[0].reshape(n_pool, PAGE, 512)
    pt_flat = page_table.reshape(-1).astype(jnp.int32)
    kc_s, vc_s = compress(pool_nsa, pt_flat, b, npg * nsub)

    nl_p = -(-(t // SEL_BLOCK) // LANES) * LANES
    nl_s = -(-(past // SEL_BLOCK) // LANES) * LANES
    nbp = t // nq_cmp
    oc_p, sb_p = _cmp_attn(qn[:t].reshape(nbp, nq_cmp, 512), kc_p, vc_p, _importance_matrix(tpg * nsub, nl_p),
                           jnp.zeros((nbp,), jnp.int32), jnp.arange(nbp, dtype=jnp.int32) * nq_cmp, nq_cmp,
                           t // SEL_BLOCK, min(N_SEL, t // SEL_BLOCK))
    qn_s = qn[t:t + b * NQS].reshape(b, NQS, 512)
    oc_s, sb_s = _cmp_attn(qn_s, kc_s, vc_s, _importance_matrix(npg * nsub, nl_s), jnp.arange(b, dtype=jnp.int32),
                           jnp.full((b,), past, jnp.int32), NQS, past // SEL_BLOCK,
                           min(N_SEL, past // SEL_BLOCK + 1) - 1)

    os_p = _sel_prompt(qn, sb_p.reshape(t, 2 * nl_p), kaug, vdup, t, tq_sel, nl_p)
    ow_p = _win_prompt(qn, kwdup, vwdup, t, tq_win)

    nsa_new3 = nsa_rows[t:t + b * NQS].reshape(b, NQS, 512)
    os_s = _sel_sample(qn_s, sb_s, nsa_new3, pool_nsa, pt_flat, _block_expand(pps_sel), b, npg, pps_sel, nl_s)
    win_new3 = win_rows[t:t + b * NQS].reshape(b, NQS, 256)
    ow_s = _win_sample(qn_s, state_nsa_win_kv[0].reshape(b, wb, 256), win_new3, b, wb)
    of_s = _fox_sample(qf[t:t + b * NQS].reshape(b, NQS, 512), fox_rows[t:t + b * NQS].reshape(b, NQS, 1024),
                       misc[t:t + b * NQS].reshape(b, NQS, LANES), cache_fox_kv[0].reshape(n_pool, PAGE, 1024),
                       cache_fox_logf[0], pt_flat, b, npg, pps_fox)

    cat = lambda p, s: _pad_rows(jnp.concatenate([p, s.reshape(b * NQS, 512)], axis=0), rp)
    x2 = _merge(x1, cat(oc_p.reshape(t, 512), oc_s), cat(os_p, os_s), cat(ow_p, ow_s), cat(o_fox_p, of_s), misc, gm,
                _gate_expand(), w_branch_nsa[0].astype(BF), w_branch_fox[0].astype(BF), w_out[0].astype(BF), tm)
    y = _ffn(x2, norm_ffn2, padc(w2_gate[0]), padc(w2_up[0]), padr(w2_down[0]), tm)

    smp = lambda a, *shape: a[t:t + b * NQS].reshape((b, NQS) + shape)[:, :ts]
    g, hd = NSA_KV_HEADS, HEAD_DIM
    logf = misc[:, 24:32]
    win_new = smp(win_rows, 2, g, hd)
    win_all_s = jnp.concatenate([state_nsa_win_kv[0], win_new], axis=1)
    keep_p = min(WINDOW, t)
    keep_s = min(WINDOW, wb + ts)
    return (y[:t][None], smp(y, d),
            nsa_rows[:t].reshape(1, 1, t, 4, g, hd), smp(nsa_rows, 4, g, hd)[None],
            fox_rows[:t].reshape(1, 1, t, 2, FOX_HEADS, hd), smp(fox_rows, 2, FOX_HEADS, hd)[None],
            logf[:t].reshape(1, 1, t, FOX_HEADS), smp(logf, FOX_HEADS)[None],
            win_rows[t - keep_p:t].reshape(1, 1, keep_p, 2, g, hd), win_all_s[:, wb + ts - keep_s:][None])
```

```python
import functools
import math

import numpy as np
import jax
import jax.numpy as jnp
from jax import lax
from jax.experimental import pallas as pl
from jax.experimental.pallas import tpu as pltpu

F32 = jnp.float32
BF = jnp.bfloat16

HEAD_DIM = 64
NSA_HEADS = 8
NSA_KV_HEADS = 2
FOX_HEADS = 8
ROT_DIM = 16
ROPE_THETA = 500000.0
CMP_BLOCK = 32
CMP_STRIDE = 16
CMP_HIDDEN = 128
SEL_BLOCK = 64
N_SEL = 16
WINDOW = 512
PAGE = 128
EPS = 1e-6
FORCE_SCORE = 1e6
SCALE = 0.125
LOG2E = 1.4426950408889634

LANES = 128
NQS = 8
NEG = -1e30
SEL_OFF = -float(2 ** 30)
VMEM_LIMIT = 56 * 1024 * 1024

NT = (((1,), (1,)), ((), ()))


def _cparams(*sem):
    return pltpu.CompilerParams(dimension_semantics=sem, vmem_limit_bytes=VMEM_LIMIT)


def _const_spec(shape):
    return pl.BlockSpec(shape, lambda *_: (0,) * len(shape), pipeline_mode=pl.Buffered(1))


def _split3(a):
    hi = a.astype(BF)
    r1 = a - hi.astype(F32)
    mid = r1.astype(BF)
    lo = (r1 - mid.astype(F32)).astype(BF)
    return hi, mid, lo


def _dot3(a, m):
    hi, mid, lo = _split3(a)
    return (jnp.dot(hi, m, preferred_element_type=F32) + jnp.dot(mid, m, preferred_element_type=F32)
            + jnp.dot(lo, m, preferred_element_type=F32))


def _rms_rows(x, gain):
    return x * lax.rsqrt(jnp.mean(x * x, axis=-1, keepdims=True) + EPS) * gain


def _ffn_body(x_ref, n_ref, wg_ref, wu_ref, wd_ref, o_ref):
    x = x_ref[...]
    h = _rms_rows(x, n_ref[...]).astype(BF)
    g = jnp.dot(h, wg_ref[...], preferred_element_type=F32)
    u = jnp.dot(h, wu_ref[...], preferred_element_type=F32)
    a = (g * jax.nn.sigmoid(g) * u).astype(BF)
    o_ref[...] = x + 0.5 * jnp.dot(a, wd_ref[...], preferred_element_type=F32)


def _ffn(x, n, wg, wu, wd, tm):
    rp, d = x.shape
    f = wg.shape[1]
    row = pl.BlockSpec((tm, d), lambda i: (i, 0))
    return pl.pallas_call(
        _ffn_body, grid=(rp // tm,),
        in_specs=[row, _const_spec((1, d)), _const_spec((d, f)), _const_spec((d, f)), _const_spec((f, d))],
        out_specs=row, out_shape=jax.ShapeDtypeStruct((rp, d), F32),
        compiler_params=_cparams("parallel"), name="ffn")(x, n, wg, wu, wd)


def _proj_body(x_ref, nm_ref, w_ref, seg_ref, gq_ref, gks_ref, gkw_ref, gfq_ref, gfk_ref, bm_ref, c_ref, s_ref,
               qn_ref, nsa_ref, win_ref, qf_ref, fox_ref, gm_ref, misc_ref, foxb_ref, kaug_ref, vdup_ref, kwdup_ref,
               vwdup_ref, *, tm, d):
    x = x_ref[...]
    h = _rms_rows(x, nm_ref[...]).astype(BF)
    z = jnp.dot(h, w_ref[...], preferred_element_type=F32)
    cosv = c_ref[...]
    sinv = s_ref[...]
    seg = seg_ref[...]
    lane = lax.broadcasted_iota(jnp.int32, (tm, LANES), 1)
    first8 = (lane % HEAD_DIM) < (ROT_DIM // 2)
    low = lane < HEAD_DIM

    def hnorm(v, gain):
        zz = v * v
        hi = zz.astype(BF)
        lo = (zz - hi.astype(F32)).astype(BF)
        ms = jnp.dot(hi, seg, preferred_element_type=F32) + jnp.dot(lo, seg, preferred_element_type=F32)
        return v * lax.rsqrt(ms + EPS) * gain

    def rope(v):
        partner = jnp.where(first8, pltpu.roll(v, LANES - ROT_DIM // 2, 1), pltpu.roll(v, ROT_DIM // 2, 1))
        return v * cosv + partner * sinv

    def dup(v):
        r = pltpu.roll(v, HEAD_DIM, 1)
        return jnp.where(low, v, r), jnp.where(low, r, v)

    def chunk(off, c):
        return z[:, off + LANES * c: off + LANES * (c + 1)]

    o_q, o_kv, o_f, o_gm = 0, 512, 1280, 2816
    o_misc = o_gm + 2 * d
    for c in range(4):
        qn_ref[:, LANES * c: LANES * (c + 1)] = rope(hnorm(chunk(o_q, c), gq_ref[...]))
        qf_ref[:, LANES * c: LANES * (c + 1)] = hnorm(chunk(o_f, c), gfq_ref[...])
        fk = hnorm(chunk(o_f + 512, c), gfk_ref[...])
        fox_ref[:, LANES * c: LANES * (c + 1)] = fk
        foxb_ref[:, LANES * c: LANES * (c + 1)] = fk.astype(BF)
    fv = z[:, o_f + 1024: o_f + 1536]
    fox_ref[:, 512:1024] = fv
    foxb_ref[:, 512:1024] = fv.astype(BF)
    nsa_ref[:, 0:256] = z[:, o_kv: o_kv + 256]
    ks = rope(hnorm(chunk(o_kv, 2), gks_ref[...]))
    vs = chunk(o_kv, 3)
    kw = rope(hnorm(chunk(o_kv, 4), gkw_ref[...]))
    vw = chunk(o_kv, 5)
    nsa_ref[:, 256:384] = ks
    nsa_ref[:, 384:512] = vs
    win_ref[:, 0:128] = kw
    win_ref[:, 128:256] = vw
    gm_ref[...] = jax.nn.sigmoid(z[:, o_gm: o_gm + 2 * d])
    zm = z[:, o_misc: o_misc + LANES] + bm_ref[...]
    logsig = jnp.minimum(zm, 0.0) - jnp.log1p(jnp.exp(-jnp.abs(zm)))
    misc_ref[...] = jnp.where(lane < 3 * NSA_HEADS, jax.nn.sigmoid(zm), logsig)
    pos = pl.program_id(0) * tm + lax.broadcasted_iota(jnp.int32, (tm, 1), 0)
    onehot = jnp.where((pos // SEL_BLOCK) % LANES == lane, 1.0, 0.0).astype(BF)
    k0, k1 = dup(ks * SCALE)
    v0, v1 = dup(vs)
    kw0, kw1 = dup(kw * SCALE)
    vw0, vw1 = dup(vw)
    for g, (kk, vv, kkw, vvw) in enumerate(((k0, v0, kw0, vw0), (k1, v1, kw1, vw1))):
        kaug_ref[g, :, 0:LANES] = onehot
        kaug_ref[g, :, LANES:2 * LANES] = kk.astype(BF)
        vdup_ref[g] = vv.astype(BF)
        kwdup_ref[g] = kkw.astype(BF)
        vwdup_ref[g] = vvw.astype(BF)


def _proj(x, nm, wa, seg, gq, gks, gkw, gfq, gfk, bm, cosv, sinv, tm):
    rp, d = x.shape
    n = wa.shape[1]
    row = lambda w: pl.BlockSpec((tm, w), lambda i: (i, 0))
    grp = lambda w: pl.BlockSpec((2, tm, w), lambda i: (0, i, 0))
    g128 = _const_spec((1, LANES))
    outs = [(512, F32), (512, F32), (256, F32), (512, F32), (1024, F32), (2 * d, F32), (LANES, F32), (1024, BF)]
    out_shape = [jax.ShapeDtypeStruct((rp, w), dt) for w, dt in outs]
    out_shape += [jax.ShapeDtypeStruct((2, rp, w), BF) for w in (256, 128, 128, 128)]
    out_specs = [row(w) for w, _ in outs] + [grp(w) for w in (256, 128, 128, 128)]
    return pl.pallas_call(
        functools.partial(_proj_body, tm=tm, d=d), grid=(rp // tm,),
        in_specs=[row(d), _const_spec((1, d)), _const_spec((d, n)), _const_spec((LANES, LANES)),
                  g128, g128, g128, g128, g128, g128, row(LANES), row(LANES)],
        out_specs=out_specs, out_shape=out_shape,
        compiler_params=_cparams("parallel"), name="proj")(x, nm, wa, seg, gq, gks, gkw, gfq, gfk, bm, cosv, sinv)


def _merge_body(x_ref, oc_ref, os_ref, ow_ref, of_ref, misc_ref, gm_ref, e_ref, wbn_ref, wbf_ref, wo_ref, o_ref, *, d):
    hi, mid, lo = _split3(misc_ref[...])

    def gate(i):
        e = e_ref[i]
        return (jnp.dot(hi, e, preferred_element_type=F32) + jnp.dot(mid, e, preferred_element_type=F32)
                + jnp.dot(lo, e, preferred_element_type=F32))

    o_nsa = gate(0) * oc_ref[...] + gate(1) * os_ref[...] + gate(2) * ow_ref[...]
    a = jnp.dot(o_nsa.astype(BF), wbn_ref[...], preferred_element_type=F32)
    b = jnp.dot(of_ref[...].astype(BF), wbf_ref[...], preferred_element_type=F32)
    gm = gm_ref[...]
    mixed = gm[:, :d] * a + gm[:, d:] * b
    o_ref[...] = x_ref[...] + jnp.dot(mixed.astype(BF), wo_ref[...], preferred_element_type=F32)


def _merge(x, oc, os_, ow, of, misc, gm, e, wbn, wbf, wo, tm):
    rp, d = x.shape
    row = lambda w: pl.BlockSpec((tm, w), lambda i: (i, 0))
    return pl.pallas_call(
        functools.partial(_merge_body, d=d), grid=(rp // tm,),
        in_specs=[row(d), row(512), row(512), row(512), row(512), row(LANES), row(2 * d),
                  _const_spec((3, LANES, 512)), _const_spec((512, d)), _const_spec((512, d)), _const_spec((d, d))],
        out_specs=row(d), out_shape=jax.ShapeDtypeStruct((rp, d), F32),
        compiler_params=_cparams("parallel"), name="merge")(x, oc, os_, ow, of, misc, gm, e, wbn, wbf, wo)


def _cumsum_body(m_ref, u_ref, o_ref, carry_ref, *, tb):
    @pl.when(pl.program_id(0) == 0)
    def _():
        carry_ref[...] = jnp.zeros_like(carry_ref)

    lft = m_ref[...].T
    a = lft[3 * NSA_HEADS: 3 * NSA_HEADS + FOX_HEADS, :]
    c = _dot3(a, u_ref[...]) + carry_ref[:, 0:1]
    o_ref[...] = c * LOG2E
    carry_ref[...] = jnp.broadcast_to(c[:, tb - 1: tb], carry_ref.shape)


def _cumsum_t(misc, t, tb):
    u = jnp.asarray(np.triu(np.ones((tb, tb), np.float32)), BF)
    return pl.pallas_call(
        functools.partial(_cumsum_body, tb=tb), grid=(t // tb,),
        in_specs=[pl.BlockSpec((tb, LANES), lambda i: (i, 0)), _const_spec((tb, tb))],
        out_specs=pl.BlockSpec((FOX_HEADS, tb), lambda i: (0, i)),
        out_shape=jax.ShapeDtypeStruct((FOX_HEADS, t), F32),
        scratch_shapes=[pltpu.VMEM((FOX_HEADS, LANES), F32)],
        compiler_params=_cparams("arbitrary"), name="fox_cumsum")(misc, u)


def _tri_steps(n):
    qi = np.concatenate([np.full(i + 1, i, np.int32) for i in range(n)])
    kj = np.concatenate([np.arange(i + 1, dtype=np.int32) for i in range(n)])
    return jnp.asarray(qi), jnp.asarray(kj)


def _chunk_update(s, v, m_ref, l_ref, acc_ref, idx, r0, rb):
    rows = slice(r0, r0 + rb)
    m_prev = m_ref[idx, rows, :]
    m_new = jnp.maximum(m_prev, jnp.max(s, axis=1, keepdims=True))
    alpha = jnp.exp2(m_prev - m_new)
    p = jnp.exp2(s - jnp.tile(m_new, (1, s.shape[1] // LANES)))
    l_ref[idx, rows, :] = alpha * l_ref[idx, rows, :] + jnp.sum(p, axis=1, keepdims=True)
    acc_ref[idx, rows, :] = alpha * acc_ref[idx, rows, :] + jnp.dot(p.astype(BF), v, preferred_element_type=F32)
    m_ref[idx, rows, :] = m_new


def _causal_chunk(s, r0):
    r = r0 + lax.broadcasted_iota(jnp.int32, s.shape, 0)
    c = lax.broadcasted_iota(jnp.int32, s.shape, 1)
    return jnp.where(c <= r, s, NEG)


def _foxp_body(qi_ref, kj_ref, q_ref, k_ref, v_ref, c_ref, o_ref, qm_ref, m_ref, l_ref, acc_ref, *, tb, rb):
    sid = pl.program_id(1)
    i = qi_ref[sid]
    j = kj_ref[sid]
    low = lax.broadcasted_iota(jnp.int32, (tb, LANES), 1) < HEAD_DIM

    @pl.when(j == 0)
    def _():
        q = q_ref[...] * (SCALE * LOG2E)
        qm_ref[0] = jnp.where(low, q, 0.0).astype(BF)
        qm_ref[1] = jnp.where(low, 0.0, q).astype(BF)
        m_ref[...] = jnp.full_like(m_ref, NEG)
        l_ref[...] = jnp.zeros_like(l_ref)
        acc_ref[...] = jnp.zeros_like(acc_ref)

    def step(diag):
        for hh in range(2):
            for r0 in range(0, tb, rb):
                nk = r0 + rb if diag else tb
                s = (lax.dot_general(qm_ref[hh, r0:r0 + rb, :], k_ref[0:nk, :], NT, preferred_element_type=F32)
                     - c_ref[0, hh:hh + 1, 0:nk])
                if diag:
                    s = _causal_chunk(s, r0)
                _chunk_update(s, v_ref[0:nk, :], m_ref, l_ref, acc_ref, hh, r0, rb)

    pl.when(j < i)(lambda: step(False))

    @pl.when(j == i)
    def _():
        step(True)
        o_ref[...] = jnp.where(low, acc_ref[0] / l_ref[0], acc_ref[1] / l_ref[1])


def _fox_prompt(qf, foxb, ct, t, tb, rb):
    n = t // tb
    qi, kj = _tri_steps(n)
    npair = FOX_HEADS // 2
    gs = pltpu.PrefetchScalarGridSpec(
        num_scalar_prefetch=2, grid=(npair, qi.shape[0]),
        in_specs=[pl.BlockSpec((tb, LANES), lambda p, s, qi, kj: (qi[s], p)),
                  pl.BlockSpec((tb, LANES), lambda p, s, qi, kj: (kj[s], p)),
                  pl.BlockSpec((tb, LANES), lambda p, s, qi, kj: (kj[s], npair + p)),
                  pl.BlockSpec((1, 2, tb), lambda p, s, qi, kj: (p, 0, kj[s]))],
        out_specs=pl.BlockSpec((tb, LANES), lambda p, s, qi, kj: (qi[s], p)),
        scratch_shapes=[pltpu.VMEM((2, tb, LANES), BF), pltpu.VMEM((2, tb, LANES), F32),
                        pltpu.VMEM((2, tb, LANES), F32), pltpu.VMEM((2, tb, LANES), F32)])
    return pl.pallas_call(
        functools.partial(_foxp_body, tb=tb, rb=rb), grid_spec=gs,
        out_shape=jax.ShapeDtypeStruct((t, 512), F32),
        compiler_params=_cparams("parallel", "arbitrary"), name="fox_prompt")(qi, kj, qf, foxb, foxb, ct)


def _cmp_accumulate(rows_k, rows_v, wk_ref, wv_ref, abk_ref, abv_ref):
    acck = jnp.zeros(abk_ref.shape, F32)
    accv = jnp.zeros(abv_ref.shape, F32)
    for l in range(CMP_STRIDE):
        acck = acck + jnp.dot(rows_k(l).astype(BF), wk_ref[l], preferred_element_type=F32)
        accv = accv + jnp.dot(rows_v(l).astype(BF), wv_ref[l], preferred_element_type=F32)
    abk_ref[...] = acck
    abv_ref[...] = accv


def _cmpab_body(pt_ref, *refs, pps):
    del pt_ref
    kp, vp = refs[:pps], refs[pps:2 * pps]
    wk_ref, wv_ref, abk_ref, abv_ref = refs[2 * pps:]
    nsub = PAGE // CMP_STRIDE
    _cmp_accumulate(
        lambda l: jnp.concatenate([r[0, pl.ds(l, nsub, stride=CMP_STRIDE), :] for r in kp], axis=0),
        lambda l: jnp.concatenate([r[0, pl.ds(l, nsub, stride=CMP_STRIDE), :] for r in vp], axis=0),
        wk_ref, wv_ref, abk_ref, abv_ref)


def _cmpabt_body(pt_ref, *refs, pps):
    del pt_ref
    pages = refs[:pps]
    wk_ref, wv_ref, abk_ref, abv_ref, xk_ref, xv_ref = refs[pps:]
    for k, r in enumerate(pages):
        xk_ref[PAGE * k: PAGE * (k + 1), :] = r[0, 0].reshape(2 * HEAD_DIM, PAGE).T
        xv_ref[PAGE * k: PAGE * (k + 1), :] = r[0, 1].reshape(2 * HEAD_DIM, PAGE).T
    nsub = pps * PAGE // CMP_STRIDE
    _cmp_accumulate(lambda l: xk_ref[pl.ds(l, nsub, stride=CMP_STRIDE), :],
                    lambda l: xv_ref[pl.ds(l, nsub, stride=CMP_STRIDE), :], wk_ref, wv_ref, abk_ref, abv_ref)


def _cmp_ab_t(pool_t, pt_flat, wk, wv, pps):
    npg = pt_flat.shape[0]
    rows = pps * PAGE // CMP_STRIDE
    pspec = [pl.BlockSpec((1, 2, 2, HEAD_DIM, PAGE), lambda s, pt, k=k: (pt[s * pps + k], 0, 0, 0, 0)) for k in range(pps)]
    wspec = pl.BlockSpec((CMP_STRIDE, LANES, 512), lambda s, pt: (0, 0, 0), pipeline_mode=pl.Buffered(1))
    ospec = pl.BlockSpec((rows, 512), lambda s, pt: (s, 0))
    gs = pltpu.PrefetchScalarGridSpec(num_scalar_prefetch=1, grid=(npg // pps,), in_specs=pspec + [wspec, wspec],
                                      out_specs=[ospec, ospec],
                                      scratch_shapes=[pltpu.VMEM((pps * PAGE, LANES), F32), pltpu.VMEM((pps * PAGE, LANES), F32)])
    shp = jax.ShapeDtypeStruct((npg * PAGE // CMP_STRIDE, 512), F32)
    return pl.pallas_call(functools.partial(_cmpabt_body, pps=pps), grid_spec=gs, out_shape=[shp, shp],
                          compiler_params=_cparams("parallel"), name="cmp_ab_t")(pt_flat, *([pool_t] * pps), wk, wv)


def _cmp_ab(pool, pt_flat, wk, wv, pps):
    npg = pt_flat.shape[0]
    nsub = PAGE // CMP_STRIDE
    rows = pps * nsub
    kspec = [pl.BlockSpec((1, PAGE, LANES), lambda s, pt, k=k: (pt[s * pps + k], 0, 0)) for k in range(pps)]
    vspec = [pl.BlockSpec((1, PAGE, LANES), lambda s, pt, k=k: (pt[s * pps + k], 0, 1)) for k in range(pps)]
    wspec = pl.BlockSpec((CMP_STRIDE, LANES, 512), lambda s, pt: (0, 0, 0), pipeline_mode=pl.Buffered(1))
    ospec = pl.BlockSpec((rows, 512), lambda s, pt: (s, 0))
    gs = pltpu.PrefetchScalarGridSpec(num_scalar_prefetch=1, grid=(npg // pps,),
                                      in_specs=kspec + vspec + [wspec, wspec], out_specs=[ospec, ospec])
    shp = jax.ShapeDtypeStruct((npg * nsub, 512), F32)
    return pl.pallas_call(functools.partial(_cmpab_body, pps=pps), grid_spec=gs, out_shape=[shp, shp],
                          compiler_params=_cparams("parallel"), name="cmp_ab")(
        pt_flat, *([pool] * (2 * pps)), wk, wv)


def _gelu_tanh(x):
    return 0.5 * x * (1.0 + jnp.tanh(math.sqrt(2.0 / math.pi) * (x + 0.044715 * (x * x * x))))


def _cmpfin_body(abk_ref, abv_ref, pek_ref, w1k_ref, w2k_ref, pev_ref, w1v_ref, w2v_ref, seg_ref, gk_ref, c_ref, s_ref,
                 kc_ref, vc_ref, *, nc):
    lane = lax.broadcasted_iota(jnp.int32, (nc, LANES), 1)
    first8 = (lane % HEAD_DIM) < (ROT_DIM // 2)
    seg = seg_ref[...]
    cosv = c_ref[...]
    sinv = s_ref[...]

    def summaries(ab_ref, pe_ref, w1_ref, w2_ref, g):
        bias = jnp.dot(pe_ref[...].astype(BF), w1_ref[...].astype(BF), preferred_element_type=F32)[0:1, :]
        a = ab_ref[:, 256 * g: 256 * g + LANES]
        b = ab_ref[:, 256 * g + LANES: 256 * g + 2 * LANES]
        hid = _gelu_tanh(a + pltpu.roll(b, nc - 1, 0) + bias)
        return jnp.dot(hid.astype(BF), w2_ref[...], preferred_element_type=F32)

    for g in range(NSA_KV_HEADS):
        k = summaries(abk_ref, pek_ref, w1k_ref, w2k_ref, g)
        zz = k * k
        hi = zz.astype(BF)
        lo = (zz - hi.astype(F32)).astype(BF)
        ms = jnp.dot(hi, seg, preferred_element_type=F32) + jnp.dot(lo, seg, preferred_element_type=F32)
        k = k * lax.rsqrt(ms + EPS) * gk_ref[...]
        partner = jnp.where(first8, pltpu.roll(k, LANES - ROT_DIM // 2, 1), pltpu.roll(k, ROT_DIM // 2, 1))
        k = k * cosv + partner * sinv
        kc_ref[0, :, LANES * g: LANES * (g + 1)] = (k * SCALE).astype(BF)
        vc_ref[0, :, LANES * g: LANES * (g + 1)] = summaries(abv_ref, pev_ref, w1v_ref, w2v_ref, g).astype(BF)


def _cmp_fin(abk, abv, pek, w1k, w2k, pev, w1v, w2v, seg, gk, cosc, sinc, nseq, nc):
    ab = pl.BlockSpec((nc, 512), lambda b: (b, 0))
    kdim = CMP_BLOCK * HEAD_DIM
    out = pl.BlockSpec((1, nc, 256), lambda b: (b, 0, 0))
    shp = jax.ShapeDtypeStruct((nseq, nc, 256), BF)
    return pl.pallas_call(
        functools.partial(_cmpfin_body, nc=nc), grid=(nseq,),
        in_specs=[ab, ab, _const_spec((8, kdim)), _const_spec((kdim, CMP_HIDDEN)), _const_spec((CMP_HIDDEN, LANES)),
                  _const_spec((8, kdim)), _const_spec((kdim, CMP_HIDDEN)), _const_spec((CMP_HIDDEN, LANES)),
                  _const_spec((LANES, LANES)), _const_spec((1, LANES)), _const_spec((nc, LANES)), _const_spec((nc, LANES))],
        out_specs=[out, out], out_shape=[shp, shp],
        compiler_params=_cparams("parallel"), name="cmp_fin")(abk, abv, pek, w1k, w2k, pev, w1v, w2v, seg, gk, cosc, sinc)


def _cmpattn_body(seq_ref, pos_ref, q_ref, kc_ref, vc_ref, m_ref, oc_ref, sb_ref, *, nq, nc, nl, nblk, k_pick):
    del seq_ref
    pos0 = pos_ref[pl.program_id(0)]
    q = q_ref[0]
    lane = lax.broadcasted_iota(jnp.int32, (nq, LANES), 1)
    low = lane < HEAD_DIM
    trow = pos0 + lax.broadcasted_iota(jnp.int32, (nq, 1), 0)
    t4 = jnp.concatenate([trow] * 4, axis=0)
    posc = CMP_STRIDE * lax.broadcasted_iota(jnp.int32, (1, nc), 1) + (CMP_BLOCK - 1)
    blk = lax.broadcasted_iota(jnp.int32, (1, nl), 1)
    blkf = blk.astype(F32)
    valid = posc <= t4
    mm = m_ref[...]
    scores = []
    for g in range(NSA_KV_HEADS):
        rows = []
        for hh in range(4):
            h = 4 * g + hh
            pr = q[:, LANES * (h // 2): LANES * (h // 2 + 1)]
            rows.append(jnp.where(low, pr, 0.0) if h % 2 == 0 else jnp.where(low, 0.0, pr))
        qg = jnp.concatenate(rows, axis=0).astype(BF)
        s = lax.dot_general(qg, kc_ref[0, :, LANES * g: LANES * (g + 1)], NT, preferred_element_type=F32)
        s = jnp.where(valid, s, NEG)
        m = jnp.max(s, axis=1, keepdims=True)
        e = jnp.where(valid, jnp.exp(s - m), 0.0)
        dsum = jnp.sum(e, axis=1, keepdims=True)
        p = e / jnp.where(dsum > 0, dsum, 1.0)
        o = jnp.dot(p.astype(BF), vc_ref[0, :, LANES * g: LANES * (g + 1)], preferred_element_type=F32)
        for pj in range(2):
            oc_ref[0, :, 256 * g + LANES * pj: 256 * g + LANES * (pj + 1)] = jnp.where(
                low, o[2 * pj * nq: (2 * pj + 1) * nq], o[(2 * pj + 1) * nq: (2 * pj + 2) * nq])
        imp = p[0:nq] + p[nq:2 * nq] + p[2 * nq:3 * nq] + p[3 * nq:4 * nq]
        ps = _dot3(imp, mm)
        cur = trow // SEL_BLOCK
        force = (blk == 0) | (blk == cur) | (blk == cur - 1)
        score = jnp.where(force, FORCE_SCORE, jnp.where(blk * SEL_BLOCK <= trow, ps, -1.0))
        scores.append(jnp.where(blk < nblk, score, -1.0))

    def pick(_, carry):
        sc, sel = carry
        mx = jnp.max(sc, axis=1, keepdims=True)
        idx = jnp.min(jnp.where(sc == mx, blkf, 1e9), axis=1, keepdims=True)
        hit = blkf == idx
        return jnp.where(hit, -2.0, sc), jnp.where(hit, 1.0, sel)

    _, sel = lax.fori_loop(0, k_pick, pick, (jnp.concatenate(scores, axis=0), jnp.zeros((2 * nq, nl), F32)))
    for g in range(NSA_KV_HEADS):
        sb_ref[0, :, nl * g: nl * (g + 1)] = jnp.where(sel[nq * g: nq * (g + 1)] > 0, 0.0, SEL_OFF).astype(BF)


def _cmp_attn(q3, kcd, vcd, mmat, seq_of, pos0, nq, nblk, k_pick):
    nb = q3.shape[0]
    nc = kcd.shape[1]
    nl = mmat.shape[1]
    gs = pltpu.PrefetchScalarGridSpec(
        num_scalar_prefetch=2, grid=(nb,),
        in_specs=[pl.BlockSpec((1, nq, 512), lambda b, sq, ps: (b, 0, 0)),
                  pl.BlockSpec((1, nc, 256), lambda b, sq, ps: (sq[b], 0, 0)),
                  pl.BlockSpec((1, nc, 256), lambda b, sq, ps: (sq[b], 0, 0)),
                  pl.BlockSpec((nc, nl), lambda b, sq, ps: (0, 0), pipeline_mode=pl.Buffered(1))],
        out_specs=[pl.BlockSpec((1, nq, 512), lambda b, sq, ps: (b, 0, 0)),
                   pl.BlockSpec((1, nq, 2 * nl), lambda b, sq, ps: (b, 0, 0))])
    return pl.pallas_call(
        functools.partial(_cmpattn_body, nq=nq, nc=nc, nl=nl, nblk=nblk, k_pick=k_pick), grid_spec=gs,
        out_shape=[jax.ShapeDtypeStruct((nb, nq, 512), F32), jax.ShapeDtypeStruct((nb, nq, 2 * nl), BF)],
        compiler_params=_cparams("parallel"), name="cmp_attn")(seq_of, pos0, q3, kcd, vcd, mmat)


def _group_rows(q, low, width_tail=None):
    rows = []
    for hh in range(4):
        pr = q[:, LANES * (hh // 2): LANES * (hh // 2 + 1)]
        rows.append(jnp.where(low, pr, 0.0) if hh % 2 == 0 else jnp.where(low, 0.0, pr))
    return rows


def _store_group(o_ref, o, low, n):
    for pj in range(2):
        o_ref[:, LANES * pj: LANES * (pj + 1)] = jnp.where(low, o[2 * pj * n: (2 * pj + 1) * n],
                                                           o[(2 * pj + 1) * n: (2 * pj + 2) * n])


def _selp_body(qi_ref, kj_ref, q_ref, sb_ref, ka_ref, v_ref, o_ref, qm_ref, m_ref, l_ref, acc_ref, *, tq, rb):
    sid = pl.program_id(1)
    i = qi_ref[sid]
    j = kj_ref[sid]
    low = lax.broadcasted_iota(jnp.int32, (tq, LANES), 1) < HEAD_DIM

    @pl.when(j == 0)
    def _():
        for hh, r in enumerate(_group_rows(q_ref[...] * LOG2E, low)):
            qm_ref[hh] = r.astype(BF)
        m_ref[...] = jnp.full_like(m_ref, NEG)
        l_ref[...] = jnp.zeros_like(l_ref)
        acc_ref[...] = jnp.zeros_like(acc_ref)

    def step(diag):
        for hh in range(4):
            for t0 in range(0, tq, rb):
                nk = t0 + rb if diag else tq
                qa = jnp.concatenate([sb_ref[t0:t0 + rb, :], qm_ref[hh, t0:t0 + rb, :]], axis=1)
                s = lax.dot_general(qa, ka_ref[0, 0:nk, :], NT, preferred_element_type=F32)
                if diag:
                    s = _causal_chunk(s, t0)
                _chunk_update(s, v_ref[0, 0:nk, :], m_ref, l_ref, acc_ref, 0, hh * tq + t0, rb)

    pl.when(j < i)(lambda: step(False))

    @pl.when(j == i)
    def _():
        step(True)
        _store_group(o_ref, acc_ref[0] / l_ref[0], low, tq)


def _sel_prompt(qn, selb, kaug, vdup, t, tq, rb, nl):
    n = t // tq
    qi, kj = _tri_steps(n)
    nsup = nl // LANES
    per_sup = LANES * SEL_BLOCK // tq
    gs = pltpu.PrefetchScalarGridSpec(
        num_scalar_prefetch=2, grid=(NSA_KV_HEADS, qi.shape[0]),
        in_specs=[pl.BlockSpec((tq, 256), lambda g, s, qi, kj: (qi[s], g)),
                  pl.BlockSpec((tq, LANES), lambda g, s, qi, kj: (qi[s], g * nsup + kj[s] // per_sup)),
                  pl.BlockSpec((1, tq, 256), lambda g, s, qi, kj: (g, kj[s], 0)),
                  pl.BlockSpec((1, tq, LANES), lambda g, s, qi, kj: (g, kj[s], 0))],
        out_specs=pl.BlockSpec((tq, 256), lambda g, s, qi, kj: (qi[s], g)),
        scratch_shapes=[pltpu.VMEM((4, tq, LANES), BF), pltpu.VMEM((1, 4 * tq, LANES), F32),
                        pltpu.VMEM((1, 4 * tq, LANES), F32), pltpu.VMEM((1, 4 * tq, LANES), F32)])
    return pl.pallas_call(
        functools.partial(_selp_body, tq=tq, rb=rb), grid_spec=gs,
        out_shape=jax.ShapeDtypeStruct((t, 512), F32),
        compiler_params=_cparams("parallel", "arbitrary"), name="sel_prompt")(qi, kj, qn, selb, kaug, vdup)


def _winp_body(q_ref, *refs, tq, nprev):
    nt = nprev + 1
    k_refs, v_refs, o_ref = refs[:nt], refs[nt:2 * nt], refs[2 * nt]
    i = pl.program_id(1)
    low = lax.broadcasted_iota(jnp.int32, (tq, LANES), 1) < HEAD_DIM
    qg = jnp.concatenate(_group_rows(q_ref[...], low), axis=0).astype(BF)
    k = jnp.concatenate([r[0] for r in k_refs], axis=0)
    v = jnp.concatenate([r[0] for r in v_refs], axis=0)
    s = lax.dot_general(qg, k, NT, preferred_element_type=F32)
    tpos = i * tq + lax.broadcasted_iota(jnp.int32, s.shape, 0) % tq
    kpos = (i - nprev) * tq + lax.broadcasted_iota(jnp.int32, s.shape, 1)
    ok = (kpos <= tpos) & (tpos - kpos < WINDOW) & (kpos >= 0)
    s = jnp.where(ok, s, NEG)
    m = jnp.max(s, axis=1, keepdims=True)
    e = jnp.exp(s - m)
    o = jnp.dot(e.astype(BF), v, preferred_element_type=F32) / jnp.sum(e, axis=1, keepdims=True)
    _store_group(o_ref, o, low, tq)


def _win_prompt(qn, kwdup, vwdup, t, tq):
    nprev = WINDOW // tq
    kv = [pl.BlockSpec((1, tq, LANES), lambda g, i, dd=dd: (g, jnp.maximum(i - nprev + dd, 0), 0)) for dd in range(nprev + 1)]
    return pl.pallas_call(
        functools.partial(_winp_body, tq=tq, nprev=nprev), grid=(NSA_KV_HEADS, t // tq),
        in_specs=[pl.BlockSpec((tq, 256), lambda g, i: (i, g))] + kv + kv,
        out_specs=pl.BlockSpec((tq, 256), lambda g, i: (i, g)),
        out_shape=jax.ShapeDtypeStruct((t, 512), F32),
        compiler_params=_cparams("parallel", "parallel"), name="win_prompt")(
        qn, *([kwdup] * (nprev + 1)), *([vwdup] * (nprev + 1)))


ROWS_S = NSA_HEADS * NQS
ROWS_G = ROWS_S // NSA_KV_HEADS


def _new_key_mask(shape):
    r = lax.broadcasted_iota(jnp.int32, shape, 0) % NQS
    c = lax.broadcasted_iota(jnp.int32, shape, 1)
    return c <= r


def _slab_update(s, pv_of, m_ref, l_ref, acc_ref):
    m_prev = m_ref[0]
    m_new = jnp.maximum(m_prev, jnp.max(s, axis=1, keepdims=True))
    alpha = jnp.exp(m_prev - m_new)
    p = jnp.exp(s - m_new)
    l_ref[0] = alpha * l_ref[0] + jnp.sum(p, axis=1, keepdims=True)
    acc_ref[0] = alpha * acc_ref[0] + pv_of(p)
    m_ref[0] = m_new


def _slab_finish(sn, pv_of, m_ref, l_ref, acc_ref):
    m_prev = m_ref[0]
    m_new = jnp.maximum(m_prev, jnp.max(sn, axis=1, keepdims=True))
    alpha = jnp.exp(m_prev - m_new)
    pn = jnp.exp(sn - m_new)
    l = alpha * l_ref[0] + jnp.sum(pn, axis=1, keepdims=True)
    return (alpha * acc_ref[0] + pv_of(pn)) / l


def _slab_init(m_ref, l_ref, acc_ref):
    m_ref[...] = jnp.full_like(m_ref, NEG)
    l_ref[...] = jnp.zeros_like(l_ref)
    acc_ref[...] = jnp.zeros_like(acc_ref)


def _grp(a, g):
    return a[ROWS_G * g: ROWS_G * (g + 1)]


def _sels_body(pt_ref, q_ref, sb0_ref, sb1_ref, e_ref, kn_ref, vn_ref, *refs, pps, nsteps, per_chunk):
    del pt_ref
    pages = refs[:pps]
    o_ref, m_ref, l_ref, acc_ref = refs[pps:]
    sid = pl.program_id(1)
    pl.when(sid == 0)(lambda: _slab_init(m_ref, l_ref, acc_ref))

    q = q_ref[0] * SCALE
    qb = q.astype(BF)
    slab = lambda kind, g: jnp.concatenate([r[0, kind, g] for r in pages], axis=1).astype(BF)
    sbrows = jnp.concatenate([sb0_ref[0].astype(F32)] * 4 + [sb1_ref[0].astype(F32)] * 4, axis=0).astype(BF)
    bias = jnp.dot(sbrows, e_ref[sid % per_chunk], preferred_element_type=F32)
    s = jnp.concatenate([jnp.dot(_grp(qb, g), slab(0, g), preferred_element_type=F32)
                         for g in range(NSA_KV_HEADS)], axis=0) + bias
    _slab_update(s, lambda p: jnp.concatenate(
        [lax.dot_general(_grp(p, g).astype(BF), slab(1, g), NT, preferred_element_type=F32)
         for g in range(NSA_KV_HEADS)], axis=0), m_ref, l_ref, acc_ref)

    @pl.when(sid == nsteps - 1)
    def _():
        sn = jnp.concatenate([lax.dot_general(_grp(q, g), kn_ref[0, g], NT, preferred_element_type=F32)
                              for g in range(NSA_KV_HEADS)], axis=0)
        sn = jnp.where(_new_key_mask(sn.shape), sn, NEG)
        o_ref[0] = _slab_finish(sn, lambda pn: jnp.concatenate(
            [jnp.dot(_grp(pn, g), vn_ref[0, g], preferred_element_type=F32) for g in range(NSA_KV_HEADS)], axis=0),
            m_ref, l_ref, acc_ref)


def _sel_sample(qh, selb, kn, vn, pool_t, pt_flat, eexp, b, npg, pps, nl):
    nsteps = npg // pps
    per_chunk = eexp.shape[0]
    nchunk = nl // LANES
    pg = [pl.BlockSpec((1, 2, 2, HEAD_DIM, PAGE), lambda bb, s, pt, k=k: (pt[bb * npg + s * pps + k], 1, 0, 0, 0))
          for k in range(pps)]
    row = pl.BlockSpec((1, ROWS_S, HEAD_DIM), lambda bb, s, pt: (bb, 0, 0))
    new = pl.BlockSpec((1, NSA_KV_HEADS, NQS, HEAD_DIM), lambda bb, s, pt: (bb, 0, 0, 0))
    gs = pltpu.PrefetchScalarGridSpec(
        num_scalar_prefetch=1, grid=(b, nsteps),
        in_specs=[row,
                  pl.BlockSpec((1, NQS, LANES), lambda bb, s, pt: (bb, 0, s // per_chunk)),
                  pl.BlockSpec((1, NQS, LANES), lambda bb, s, pt: (bb, 0, nchunk + s // per_chunk)),
                  pl.BlockSpec(eexp.shape, lambda bb, s, pt: (0, 0, 0), pipeline_mode=pl.Buffered(1)),
                  new, new] + pg,
        out_specs=row,
        scratch_shapes=[pltpu.VMEM((1, ROWS_S, 1), F32), pltpu.VMEM((1, ROWS_S, 1), F32),
                        pltpu.VMEM((1, ROWS_S, HEAD_DIM), F32)])
    return pl.pallas_call(
        functools.partial(_sels_body, pps=pps, nsteps=nsteps, per_chunk=per_chunk), grid_spec=gs,
        out_shape=jax.ShapeDtypeStruct((b, ROWS_S, HEAD_DIM), F32),
        compiler_params=_cparams("parallel", "arbitrary"), name="sel_sample")(
        pt_flat, qh, selb, selb, eexp, kn, vn, *([pool_t] * pps))


def _wins_body(q_ref, st_ref, kn_ref, vn_ref, o_ref, *, wb):
    q = q_ref[0] * SCALE
    qb = q.astype(BF)
    s = jnp.concatenate([jnp.dot(_grp(qb, g), st_ref[0, 0, g].astype(BF), preferred_element_type=F32)
                         for g in range(NSA_KV_HEADS)], axis=0)
    tq = lax.broadcasted_iota(jnp.int32, s.shape, 0) % NQS
    n = lax.broadcasted_iota(jnp.int32, s.shape, 1)
    s = jnp.where(n > tq + (wb - WINDOW), s, NEG)
    sn = jnp.concatenate([lax.dot_general(_grp(q, g), kn_ref[0, g], NT, preferred_element_type=F32)
                          for g in range(NSA_KV_HEADS)], axis=0)
    sn = jnp.where(_new_key_mask(sn.shape), sn, NEG)
    m = jnp.maximum(jnp.max(s, axis=1, keepdims=True), jnp.max(sn, axis=1, keepdims=True))
    e = jnp.exp(s - m)
    en = jnp.exp(sn - m)
    den = jnp.sum(e, axis=1, keepdims=True) + jnp.sum(en, axis=1, keepdims=True)
    o = jnp.concatenate(
        [lax.dot_general(_grp(e, g).astype(BF), st_ref[0, 1, g].astype(BF), NT, preferred_element_type=F32)
         + jnp.dot(_grp(en, g), vn_ref[0, g], preferred_element_type=F32) for g in range(NSA_KV_HEADS)], axis=0)
    o_ref[0] = o / den


def _win_sample(qh, state_t, kn, vn, b, wb):
    row = pl.BlockSpec((1, ROWS_S, HEAD_DIM), lambda bb: (bb, 0, 0))
    new = pl.BlockSpec((1, NSA_KV_HEADS, NQS, HEAD_DIM), lambda bb: (bb, 0, 0, 0))
    return pl.pallas_call(
        functools.partial(_wins_body, wb=wb), grid=(b,),
        in_specs=[row, pl.BlockSpec((1, 2, NSA_KV_HEADS, HEAD_DIM, wb), lambda bb: (bb, 0, 0, 0, 0)), new, new],
        out_specs=row, out_shape=jax.ShapeDtypeStruct((b, ROWS_S, HEAD_DIM), F32),
        compiler_params=_cparams("parallel"), name="win_sample")(qh, state_t, kn, vn)


def _foxs_body(pt_ref, q_ref, kn_ref, vn_ref, lfn_ref, u_ref, *refs, pps, nsteps):
    del pt_ref
    kv = refs[:pps]
    lfp = refs[pps:2 * pps]
    o_ref, m_ref, l_ref, acc_ref, carry_ref = refs[2 * pps:]
    sid = pl.program_id(1)

    @pl.when(sid == 0)
    def _():
        _slab_init(m_ref, l_ref, acc_ref)
        carry_ref[...] = jnp.zeros_like(carry_ref)

    q = q_ref[0] * SCALE
    qb = q.astype(BF)
    head = lambda a, h: a[NQS * h: NQS * (h + 1)]
    slab = lambda kind, h: jnp.concatenate([r[0, kind, h] for r in kv], axis=1).astype(BF)
    lft = jnp.concatenate([r[0] for r in lfp], axis=1)
    cum = _dot3(lft, u_ref[...]) + carry_ref[:, 0:1]
    keys = cum.shape[1]
    carry_ref[...] = jnp.broadcast_to(cum[:, keys - 1:], carry_ref.shape)
    s = jnp.concatenate([jnp.dot(head(qb, h), slab(0, h), preferred_element_type=F32)
                         - jnp.broadcast_to(cum[h:h + 1, :], (NQS, keys)) for h in range(FOX_HEADS)], axis=0)
    _slab_update(s, lambda p: jnp.concatenate(
        [lax.dot_general(head(p, h).astype(BF), slab(1, h), NT, preferred_element_type=F32)
         for h in range(FOX_HEADS)], axis=0), m_ref, l_ref, acc_ref)

    @pl.when(sid == nsteps - 1)
    def _():
        lfn = lfn_ref[0]
        col = lax.broadcasted_iota(jnp.int32, (FOX_HEADS, NQS), 1)
        cn = carry_ref[:, 0:NQS]
        for i in range(NQS):
            cn = cn + jnp.where(col >= i, lfn[:, i:i + 1], 0.0)
        sn = jnp.concatenate([lax.dot_general(head(q, h), kn_ref[0, h], NT, preferred_element_type=F32)
                              - jnp.broadcast_to(cn[h:h + 1, :], (NQS, NQS)) for h in range(FOX_HEADS)], axis=0)
        sn = jnp.where(_new_key_mask(sn.shape), sn, NEG)
        o_ref[0] = _slab_finish(sn, lambda pn: jnp.concatenate(
            [jnp.dot(head(pn, h), vn_ref[0, h], preferred_element_type=F32) for h in range(FOX_HEADS)], axis=0),
            m_ref, l_ref, acc_ref)


def _fox_sample(qh, kn, vn, lfn, pool_t, lf_t, pt_flat, b, npg, pps):
    nsteps = npg // pps
    keys = pps * PAGE
    u = jnp.asarray(np.triu(np.ones((keys, keys), np.float32)), BF)
    kvs = [pl.BlockSpec((1, 2, FOX_HEADS, HEAD_DIM, PAGE), lambda bb, s, pt, k=k: (pt[bb * npg + s * pps + k], 0, 0, 0, 0))
           for k in range(pps)]
    lfs = [pl.BlockSpec((1, FOX_HEADS, PAGE), lambda bb, s, pt, k=k: (pt[bb * npg + s * pps + k], 0, 0)) for k in range(pps)]
    row = pl.BlockSpec((1, ROWS_S, HEAD_DIM), lambda bb, s, pt: (bb, 0, 0))
    new = pl.BlockSpec((1, FOX_HEADS, NQS, HEAD_DIM), lambda bb, s, pt: (bb, 0, 0, 0))
    gs = pltpu.PrefetchScalarGridSpec(
        num_scalar_prefetch=1, grid=(b, nsteps),
        in_specs=[row, new, new, pl.BlockSpec((1, FOX_HEADS, NQS), lambda bb, s, pt: (bb, 0, 0)),
                  pl.BlockSpec((keys, keys), lambda bb, s, pt: (0, 0), pipeline_mode=pl.Buffered(1))] + kvs + lfs,
        out_specs=row,
        scratch_shapes=[pltpu.VMEM((1, ROWS_S, 1), F32), pltpu.VMEM((1, ROWS_S, 1), F32),
                        pltpu.VMEM((1, ROWS_S, HEAD_DIM), F32), pltpu.VMEM((FOX_HEADS, LANES), F32)])
    return pl.pallas_call(
        functools.partial(_foxs_body, pps=pps, nsteps=nsteps), grid_spec=gs,
        out_shape=jax.ShapeDtypeStruct((b, ROWS_S, HEAD_DIM), F32),
        compiler_params=_cparams("parallel", "arbitrary"), name="fox_sample")(
        pt_flat, qh, kn, vn, lfn, u, *([pool_t] * pps), *([lf_t] * pps))


def _rope_tables(pos):
    half = ROT_DIM // 2
    inv = ROPE_THETA ** (-(jnp.arange(half, dtype=F32) * 2.0 / ROT_DIM))
    ang = pos.astype(F32)[:, None] * inv[None, :]
    cos, sin = jnp.cos(ang), jnp.sin(ang)
    n = pos.shape[0]
    c64 = jnp.concatenate([cos, cos, jnp.ones((n, HEAD_DIM - ROT_DIM), F32)], axis=1)
    s64 = jnp.concatenate([-sin, sin, jnp.zeros((n, HEAD_DIM - ROT_DIM), F32)], axis=1)
    return jnp.tile(c64, (1, 2)), jnp.tile(s64, (1, 2))


def _importance_matrix(nc, nl):
    m = np.zeros((nc, nl), np.float32)
    per = SEL_BLOCK // CMP_STRIDE
    for n in range(nc - 1):
        m[n, n // per] += 1.0
        m[n, (n + 1) // per] += 1.0
    return jnp.asarray(m, BF)


def _gate_expand():
    e = np.zeros((3, LANES, 512), np.float32)
    for i in range(3):
        for h in range(NSA_HEADS):
            e[i, 3 * h + i, HEAD_DIM * h: HEAD_DIM * (h + 1)] = 1.0
    return jnp.asarray(e, BF)


def _block_expand(pps):
    per_step = pps * PAGE // SEL_BLOCK
    per_chunk = LANES // per_step
    e = np.zeros((per_chunk, LANES, pps * PAGE), np.float32)
    for r in range(per_chunk):
        for key in range(pps * PAGE):
            e[r, r * per_step + key // SEL_BLOCK, key] = 1.0
    return jnp.asarray(e, BF)


def _cmp_weights(w1):
    top, bot = w1[:CMP_STRIDE], w1[CMP_STRIDE:]
    wl = jnp.concatenate([top, bot], axis=2)
    z = jnp.zeros_like(wl)
    return jnp.concatenate([jnp.concatenate([wl, z], axis=2), jnp.concatenate([z, wl], axis=2)], axis=1).astype(BF)


def _pad_rows(a, rp):
    return jnp.pad(a, ((0, rp - a.shape[0]),) + ((0, 0),) * (a.ndim - 1))


def kernel(x_prompt, x_sample, cache_nsa_kv, cache_fox_kv, cache_fox_logf, state_nsa_win_kv, page_table, norm_ffn1, w1_gate, w1_up, w1_down, norm_mix, w_in, b_forget, nsa_q_norm, nsa_k_norm, cmp_pe_k, cmp_w1_k, cmp_w2_k, cmp_pe_v, cmp_w1_v, cmp_w2_v, fox_q_norm, fox_k_norm, w_branch_nsa, w_branch_fox, w_out, norm_ffn2, w2_gate, w2_up, w2_down):
    bp, t, d = x_prompt.shape
    b, ts, _ = x_sample.shape
    assert bp == 1 and norm_ffn1.shape[0] == 1 and ts <= 4
    npg = page_table.shape[1]
    past = npg * PAGE
    n_pool = cache_nsa_kv.shape[1]
    wb = state_nsa_win_kv.shape[2]
    tm = 256
    tq_fox = min(1024, t)
    tq_sel = min(1024, t)
    rb = 1024
    tq_win = 256
    nq_cmp = 128
    pps_cmp, pps_sel, pps_fox = 16, 8, 8
    assert t % PAGE == 0 and t % tq_fox == 0 and t % tq_win == 0 and WINDOW % tq_win == 0
    assert (t // PAGE) % pps_cmp == 0 and npg % pps_cmp == 0 and npg % pps_sel == 0 and npg % pps_fox == 0

    xs = jnp.pad(x_sample, ((0, 0), (0, NQS - ts), (0, 0))).reshape(b * NQS, d)
    r = t + b * NQS
    rp = -(-r // tm) * tm
    x_all = _pad_rows(jnp.concatenate([x_prompt[0], xs], axis=0), rp)
    pos = jnp.concatenate([jnp.arange(t), past + (jnp.arange(b * NQS) % NQS), jnp.zeros((rp - r,), jnp.int32)])
    cosv, sinv = _rope_tables(pos)

    f = w1_gate.shape[2]
    fp = -(-f // LANES) * LANES
    padc = lambda w: jnp.pad(w.astype(BF), ((0, 0), (0, fp - f)))
    padr = lambda w: jnp.pad(w.astype(BF), ((0, fp - f), (0, 0)))
    wi = w_in[0]
    o2, o3, o4, o5 = 1280, 1304, 2840, 2848
    wa = jnp.concatenate([wi[:, :o2], wi[:, o3:o4], wi[:, o5:], wi[:, o2:o3], wi[:, o4:o5],
                          jnp.zeros((d, LANES - 32), F32)], axis=1).astype(BF)
    bm = jnp.zeros((1, LANES), F32).at[0, 24:32].set(b_forget[0])
    g2 = lambda g: jnp.tile(g, 2)[None, :]
    seg = jnp.asarray(np.kron(np.eye(2, dtype=np.float32), np.full((HEAD_DIM, HEAD_DIM), 1.0 / HEAD_DIM, np.float32)), BF)

    x1 = _ffn(x_all, norm_ffn1, padc(w1_gate[0]), padc(w1_up[0]), padr(w1_down[0]), tm)
    (qn, nsa_rows, win_rows, qf, fox_rows, gm, misc, foxb, kaug, vdup, kwdup, vwdup) = _proj(
        x1, norm_mix, wa, seg, g2(nsa_q_norm[0]), g2(nsa_k_norm[0, 1]), g2(nsa_k_norm[0, 2]), g2(fox_q_norm[0]),
        g2(fox_k_norm[0]), bm, cosv, sinv, tm)

    ct = _cumsum_t(misc, t, min(512, t)).reshape(FOX_HEADS // 2, 2, t)
    o_fox_p = _fox_prompt(qf, foxb, ct, t, tq_fox, rb)

    wk, wv = _cmp_weights(cmp_w1_k[0]), _cmp_weights(cmp_w1_v[0])
    kdim = CMP_BLOCK * HEAD_DIM
    pe8 = lambda pe: jnp.broadcast_to(pe.reshape(1, kdim), (8, kdim))
    w2d = lambda w2: jnp.concatenate([w2, w2], axis=1).astype(BF)
    gkc = g2(nsa_k_norm[0, 0])

    def compress(ab_fn, pool, pt_flat, nseq, nc):
        abk, abv = ab_fn(pool, pt_flat, wk, wv, pps_cmp)
        posc = jnp.arange(nc) * CMP_STRIDE + (CMP_BLOCK - 1)
        cosc, sinc = _rope_tables(posc)
        return _cmp_fin(abk, abv, pe8(cmp_pe_k[0]), cmp_w1_k[0].reshape(kdim, CMP_HIDDEN), w2d(cmp_w2_k[0]),
                        pe8(cmp_pe_v[0]), cmp_w1_v[0].reshape(kdim, CMP_HIDDEN), w2d(cmp_w2_v[0]), seg, gkc, cosc, sinc,
                        nseq, nc)

    nsub = PAGE // CMP_STRIDE
    tpg = t // PAGE
    kc_p, vc_p = compress(_cmp_ab, nsa_rows[:t].reshape(tpg, PAGE, 512), jnp.arange(tpg, dtype=jnp.int32), 1, tpg * nsub)
    nsa_t = jnp.transpose(cache_nsa_kv[0], (0, 2, 3, 4, 1))
    fox_t = jnp.transpose(cache_fox_kv[0], (0, 2, 3, 4, 1))
    lf_t = jnp.transpose(cache_fox_logf[0], (0, 2, 1))
    st_t = jnp.transpose(state_nsa_win_kv[0], (0, 2, 3, 4, 1))
    pt_flat = page_table.reshape(-1).astype(jnp.int32)
    kc_s, vc_s = compress(_cmp_ab_t, nsa_t, pt_flat, b, npg * nsub)

    nl_p = -(-(t // SEL_BLOCK) // LANES) * LANES
    nl_s = -(-(past // SEL_BLOCK) // LANES) * LANES
    nbp = t // nq_cmp
    oc_p, sb_p = _cmp_attn(qn[:t].reshape(nbp, nq_cmp, 512), kc_p, vc_p, _importance_matrix(tpg * nsub, nl_p),
                           jnp.zeros((nbp,), jnp.int32), jnp.arange(nbp, dtype=jnp.int32) * nq_cmp, nq_cmp,
                           t // SEL_BLOCK, min(N_SEL, t // SEL_BLOCK))
    qn_s = qn[t:t + b * NQS].reshape(b, NQS, 512)
    oc_s, sb_s = _cmp_attn(qn_s, kc_s, vc_s, _importance_matrix(npg * nsub, nl_s), jnp.arange(b, dtype=jnp.int32),
                           jnp.full((b,), past, jnp.int32), NQS, past // SEL_BLOCK,
                           min(N_SEL, past // SEL_BLOCK + 1) - 1)

    os_p = _sel_prompt(qn, sb_p.reshape(t, 2 * nl_p), kaug, vdup, t, tq_sel, rb, nl_p)
    ow_p = _win_prompt(qn, kwdup, vwdup, t, tq_win)

    new = lambda a: a[t:t + b * NQS]
    hm = lambda a, nh: a.reshape(b, NQS, nh, HEAD_DIM).transpose(0, 2, 1, 3)
    rows_of = lambda o: o.reshape(b, NSA_HEADS, NQS, HEAD_DIM).transpose(0, 2, 1, 3).reshape(b * NQS, 512)
    qn_h = hm(new(qn), NSA_HEADS).reshape(b, ROWS_S, HEAD_DIM)
    nsa_new, win_new_r, fox_new = new(nsa_rows), new(win_rows), new(fox_rows)
    os_s = _sel_sample(qn_h, sb_s, hm(nsa_new[:, 256:384], NSA_KV_HEADS), hm(nsa_new[:, 384:512], NSA_KV_HEADS), nsa_t,
                       pt_flat, _block_expand(pps_sel), b, npg, pps_sel, nl_s)
    ow_s = _win_sample(qn_h, st_t, hm(win_new_r[:, 0:128], NSA_KV_HEADS), hm(win_new_r[:, 128:256], NSA_KV_HEADS), b, wb)
    of_s = _fox_sample(hm(new(qf), FOX_HEADS).reshape(b, ROWS_S, HEAD_DIM), hm(fox_new[:, 0:512], FOX_HEADS),
                       hm(fox_new[:, 512:1024], FOX_HEADS), new(misc)[:, 24:32].reshape(b, NQS, FOX_HEADS).transpose(0, 2, 1),
                       fox_t, lf_t, pt_flat, b, npg, pps_fox)

    cat = lambda p, s: _pad_rows(jnp.concatenate([p, s.reshape(b * NQS, 512)], axis=0), rp)
    os_s, ow_s, of_s = rows_of(os_s), rows_of(ow_s), rows_of(of_s)
    x2 = _merge(x1, cat(oc_p.reshape(t, 512), oc_s), cat(os_p, os_s), cat(ow_p, ow_s), cat(o_fox_p, of_s), misc, gm,
                _gate_expand(), w_branch_nsa[0].astype(BF), w_branch_fox[0].astype(BF), w_out[0].astype(BF), tm)
    y = _ffn(x2, norm_ffn2, padc(w2_gate[0]), padc(w2_up[0]), padr(w2_down[0]), tm)

    smp = lambda a, *shape: a[t:t + b * NQS].reshape((b, NQS) + shape)[:, :ts]
    g, hd = NSA_KV_HEADS, HEAD_DIM
    logf = misc[:, 24:32]
    win_new = smp(win_rows, 2, g, hd)
    win_all_s = jnp.concatenate([state_nsa_win_kv[0], win_new], axis=1)
    keep_p = min(WINDOW, t)
    keep_s = min(WINDOW, wb + ts)
    return (y[:t][None], smp(y, d),
            nsa_rows[:t].reshape(1, 1, t, 4, g, hd), smp(nsa_rows, 4, g, hd)[None],
            fox_rows[:t].reshape(1, 1, t, 2, FOX_HEADS, hd), smp(fox_rows, 2, FOX_HEADS, hd)[None],
            logf[:t].reshape(1, 1, t, FOX_HEADS), smp(logf, FOX_HEADS)[None],
            win_rows[t - keep_p:t].reshape(1, 1, keep_p, 2, g, hd), win_all_s[:, wb + ts - keep_s:][None])
```

```python
import functools
import math

import numpy as np
import jax
import jax.numpy as jnp
from jax import lax
from jax.experimental import pallas as pl
from jax.experimental.pallas import tpu as pltpu

F32 = jnp.float32
BF = jnp.bfloat16

HEAD_DIM = 64
NSA_HEADS = 8
NSA_KV_HEADS = 2
FOX_HEADS = 8
ROT_DIM = 16
ROPE_THETA = 500000.0
CMP_BLOCK = 32
CMP_STRIDE = 16
CMP_HIDDEN = 128
SEL_BLOCK = 64
N_SEL = 16
WINDOW = 512
PAGE = 128
EPS = 1e-6
FORCE_SCORE = 1e6
SCALE = 0.125
LOG2E = 1.4426950408889634

LANES = 128
NQS = 8
NEG = -1e30
SEL_OFF = -float(2 ** 30)
VMEM_LIMIT = 56 * 1024 * 1024

NT = (((1,), (1,)), ((), ()))


def _cparams(*sem):
    return pltpu.CompilerParams(dimension_semantics=sem, vmem_limit_bytes=VMEM_LIMIT)


def _const_spec(shape):
    return pl.BlockSpec(shape, lambda *_: (0,) * len(shape), pipeline_mode=pl.Buffered(1))


def _split3(a):
    hi = a.astype(BF)
    r1 = a - hi.astype(F32)
    mid = r1.astype(BF)
    lo = (r1 - mid.astype(F32)).astype(BF)
    return hi, mid, lo


def _dot3(a, m):
    hi, mid, lo = _split3(a)
    return (jnp.dot(hi, m, preferred_element_type=F32) + jnp.dot(mid, m, preferred_element_type=F32)
            + jnp.dot(lo, m, preferred_element_type=F32))


def _rms_rows(x, gain):
    return x * lax.rsqrt(jnp.mean(x * x, axis=-1, keepdims=True) + EPS) * gain


def _ffn_body(x_ref, n_ref, wg_ref, wu_ref, wd_ref, o_ref):
    x = x_ref[...]
    h = _rms_rows(x, n_ref[...]).astype(BF)
    g = jnp.dot(h, wg_ref[...], preferred_element_type=F32)
    u = jnp.dot(h, wu_ref[...], preferred_element_type=F32)
    a = (g * jax.nn.sigmoid(g) * u).astype(BF)
    o_ref[...] = x + 0.5 * jnp.dot(a, wd_ref[...], preferred_element_type=F32)


def _ffn(x, n, wg, wu, wd, tm):
    rp, d = x.shape
    f = wg.shape[1]
    row = pl.BlockSpec((tm, d), lambda i: (i, 0))
    return pl.pallas_call(
        _ffn_body, grid=(rp // tm,),
        in_specs=[row, _const_spec((1, d)), _const_spec((d, f)), _const_spec((d, f)), _const_spec((f, d))],
        out_specs=row, out_shape=jax.ShapeDtypeStruct((rp, d), F32),
        compiler_params=_cparams("parallel"), name="ffn")(x, n, wg, wu, wd)


def _proj_body(x_ref, nm_ref, w_ref, seg_ref, gq_ref, gks_ref, gkw_ref, gfq_ref, gfk_ref, bm_ref, c_ref, s_ref,
               qn_ref, nsa_ref, win_ref, qf_ref, fox_ref, gm_ref, misc_ref, foxb_ref, kaug_ref, vdup_ref, kwdup_ref,
               vwdup_ref, *, tm, d):
    x = x_ref[...]
    h = _rms_rows(x, nm_ref[...]).astype(BF)
    z = jnp.dot(h, w_ref[...], preferred_element_type=F32)
    cosv = c_ref[...]
    sinv = s_ref[...]
    seg = seg_ref[...]
    lane = lax.broadcasted_iota(jnp.int32, (tm, LANES), 1)
    first8 = (lane % HEAD_DIM) < (ROT_DIM // 2)
    low = lane < HEAD_DIM

    def hnorm(v, gain):
        zz = v * v
        hi = zz.astype(BF)
        lo = (zz - hi.astype(F32)).astype(BF)
        ms = jnp.dot(hi, seg, preferred_element_type=F32) + jnp.dot(lo, seg, preferred_element_type=F32)
        return v * lax.rsqrt(ms + EPS) * gain

    def rope(v):
        partner = jnp.where(first8, pltpu.roll(v, LANES - ROT_DIM // 2, 1), pltpu.roll(v, ROT_DIM // 2, 1))
        return v * cosv + partner * sinv

    def dup(v):
        r = pltpu.roll(v, HEAD_DIM, 1)
        return jnp.where(low, v, r), jnp.where(low, r, v)

    def chunk(off, c):
        return z[:, off + LANES * c: off + LANES * (c + 1)]

    o_q, o_kv, o_f, o_gm = 0, 512, 1280, 2816
    o_misc = o_gm + 2 * d
    for c in range(4):
        qn_ref[:, LANES * c: LANES * (c + 1)] = rope(hnorm(chunk(o_q, c), gq_ref[...]))
        qf_ref[:, LANES * c: LANES * (c + 1)] = hnorm(chunk(o_f, c), gfq_ref[...])
        fk = hnorm(chunk(o_f + 512, c), gfk_ref[...])
        fox_ref[:, LANES * c: LANES * (c + 1)] = fk
        foxb_ref[:, LANES * c: LANES * (c + 1)] = fk.astype(BF)
    fv = z[:, o_f + 1024: o_f + 1536]
    fox_ref[:, 512:1024] = fv
    foxb_ref[:, 512:1024] = fv.astype(BF)
    nsa_ref[:, 0:256] = z[:, o_kv: o_kv + 256]
    ks = rope(hnorm(chunk(o_kv, 2), gks_ref[...]))
    vs = chunk(o_kv, 3)
    kw = rope(hnorm(chunk(o_kv, 4), gkw_ref[...]))
    vw = chunk(o_kv, 5)
    nsa_ref[:, 256:384] = ks
    nsa_ref[:, 384:512] = vs
    win_ref[:, 0:128] = kw
    win_ref[:, 128:256] = vw
    gm_ref[...] = jax.nn.sigmoid(z[:, o_gm: o_gm + 2 * d])
    zm = z[:, o_misc: o_misc + LANES] + bm_ref[...]
    logsig = jnp.minimum(zm, 0.0) - jnp.log1p(jnp.exp(-jnp.abs(zm)))
    misc_ref[...] = jnp.where(lane < 3 * NSA_HEADS, jax.nn.sigmoid(zm), logsig)
    pos = pl.program_id(0) * tm + lax.broadcasted_iota(jnp.int32, (tm, 1), 0)
    blk = pos // SEL_BLOCK
    onehot = jnp.where(blk % LANES == lane, 1.0, 0.0)
    hot_half = (lane // HEAD_DIM) == (blk // HEAD_DIM) % 2
    k0, k1 = dup(ks * SCALE)
    v0, v1 = dup(vs)
    kw0, kw1 = dup(kw * SCALE)
    vw0, vw1 = dup(vw)
    for g, (kk, vv, kkw, vvw) in enumerate(((k0, v0, kw0, vw0), (k1, v1, kw1, vw1))):
        kaug_ref[g] = jnp.where(hot_half, onehot, kk).astype(BF)
        vdup_ref[g] = vv.astype(BF)
        kwdup_ref[g] = kkw.astype(BF)
        vwdup_ref[g] = vvw.astype(BF)


def _proj(x, nm, wa, seg, gq, gks, gkw, gfq, gfk, bm, cosv, sinv, tm):
    rp, d = x.shape
    n = wa.shape[1]
    row = lambda w: pl.BlockSpec((tm, w), lambda i: (i, 0))
    grp = lambda w: pl.BlockSpec((2, tm, w), lambda i: (0, i, 0))
    g128 = _const_spec((1, LANES))
    outs = [(512, F32), (512, F32), (256, F32), (512, F32), (1024, F32), (2 * d, F32), (LANES, F32), (1024, BF)]
    out_shape = [jax.ShapeDtypeStruct((rp, w), dt) for w, dt in outs]
    out_shape += [jax.ShapeDtypeStruct((2, rp, LANES), BF) for _ in range(4)]
    out_specs = [row(w) for w, _ in outs] + [grp(LANES) for _ in range(4)]
    return pl.pallas_call(
        functools.partial(_proj_body, tm=tm, d=d), grid=(rp // tm,),
        in_specs=[row(d), _const_spec((1, d)), _const_spec((d, n)), _const_spec((LANES, LANES)),
                  g128, g128, g128, g128, g128, g128, row(LANES), row(LANES)],
        out_specs=out_specs, out_shape=out_shape,
        compiler_params=_cparams("parallel"), name="proj")(x, nm, wa, seg, gq, gks, gkw, gfq, gfk, bm, cosv, sinv)


def _merge_body(x_ref, oc_ref, os_ref, ow_ref, of_ref, misc_ref, gm_ref, e_ref, wbn_ref, wbf_ref, wo_ref, o_ref, *, d):
    hi, mid, lo = _split3(misc_ref[...])

    def gate(i):
        e = e_ref[i]
        return (jnp.dot(hi, e, preferred_element_type=F32) + jnp.dot(mid, e, preferred_element_type=F32)
                + jnp.dot(lo, e, preferred_element_type=F32))

    o_nsa = gate(0) * oc_ref[...] + gate(1) * os_ref[...] + gate(2) * ow_ref[...]
    a = jnp.dot(o_nsa.astype(BF), wbn_ref[...], preferred_element_type=F32)
    b = jnp.dot(of_ref[...].astype(BF), wbf_ref[...], preferred_element_type=F32)
    gm = gm_ref[...]
    mixed = gm[:, :d] * a + gm[:, d:] * b
    o_ref[...] = x_ref[...] + jnp.dot(mixed.astype(BF), wo_ref[...], preferred_element_type=F32)


def _merge(x, oc, os_, ow, of, misc, gm, e, wbn, wbf, wo, tm):
    rp, d = x.shape
    row = lambda w: pl.BlockSpec((tm, w), lambda i: (i, 0))
    return pl.pallas_call(
        functools.partial(_merge_body, d=d), grid=(rp // tm,),
        in_specs=[row(d), row(512), row(512), row(512), row(512), row(LANES), row(2 * d),
                  _const_spec((3, LANES, 512)), _const_spec((512, d)), _const_spec((512, d)), _const_spec((d, d))],
        out_specs=row(d), out_shape=jax.ShapeDtypeStruct((rp, d), F32),
        compiler_params=_cparams("parallel"), name="merge")(x, oc, os_, ow, of, misc, gm, e, wbn, wbf, wo)


def _cumsum_body(m_ref, u_ref, o_ref, carry_ref, *, tb):
    @pl.when(pl.program_id(0) == 0)
    def _():
        carry_ref[...] = jnp.zeros_like(carry_ref)

    lft = m_ref[...].T
    a = lft[3 * NSA_HEADS: 3 * NSA_HEADS + FOX_HEADS, :]
    c = _dot3(a, u_ref[...]) + carry_ref[:, 0:1]
    o_ref[...] = c * LOG2E
    carry_ref[...] = jnp.broadcast_to(c[:, tb - 1: tb], carry_ref.shape)


def _cumsum_t(misc, t, tb):
    u = jnp.asarray(np.triu(np.ones((tb, tb), np.float32)), BF)
    return pl.pallas_call(
        functools.partial(_cumsum_body, tb=tb), grid=(t // tb,),
        in_specs=[pl.BlockSpec((tb, LANES), lambda i: (i, 0)), _const_spec((tb, tb))],
        out_specs=pl.BlockSpec((FOX_HEADS, tb), lambda i: (0, i)),
        out_shape=jax.ShapeDtypeStruct((FOX_HEADS, t), F32),
        scratch_shapes=[pltpu.VMEM((FOX_HEADS, LANES), F32)],
        compiler_params=_cparams("arbitrary"), name="fox_cumsum")(misc, u)


def _tri_steps(n):
    qi = np.concatenate([np.full(i + 1, i, np.int32) for i in range(n)])
    kj = np.concatenate([np.arange(i + 1, dtype=np.int32) for i in range(n)])
    return jnp.asarray(qi), jnp.asarray(kj)


def _chunk_update(s, v, m_ref, l_ref, acc_ref, idx, r0, rb):
    rows = slice(r0, r0 + rb)
    m_prev = m_ref[idx, rows, :]
    m_new = jnp.maximum(m_prev, jnp.max(s, axis=1, keepdims=True))
    alpha = jnp.exp2(m_prev - m_new)
    p = jnp.exp2(s - jnp.tile(m_new, (1, s.shape[1] // LANES)))
    l_ref[idx, rows, :] = alpha * l_ref[idx, rows, :] + jnp.sum(p, axis=1, keepdims=True)
    acc_ref[idx, rows, :] = alpha * acc_ref[idx, rows, :] + jnp.dot(p.astype(BF), v, preferred_element_type=F32)
    m_ref[idx, rows, :] = m_new


def _causal_chunk(s, r0):
    r = r0 + lax.broadcasted_iota(jnp.int32, s.shape, 0)
    c = lax.broadcasted_iota(jnp.int32, s.shape, 1)
    return jnp.where(c <= r, s, NEG)


def _foxp_body(qi_ref, kj_ref, q_ref, k_ref, v_ref, c_ref, o_ref, qm_ref, m_ref, l_ref, acc_ref, *, tb, rb):
    sid = pl.program_id(1)
    i = qi_ref[sid]
    j = kj_ref[sid]
    low = lax.broadcasted_iota(jnp.int32, (tb, LANES), 1) < HEAD_DIM

    @pl.when(j == 0)
    def _():
        q = q_ref[...] * (SCALE * LOG2E)
        qm_ref[0] = jnp.where(low, q, 0.0).astype(BF)
        qm_ref[1] = jnp.where(low, 0.0, q).astype(BF)
        m_ref[...] = jnp.full_like(m_ref, NEG)
        l_ref[...] = jnp.zeros_like(l_ref)
        acc_ref[...] = jnp.zeros_like(acc_ref)

    def step(diag):
        for hh in range(2):
            for r0 in range(0, tb, rb):
                nk = r0 + rb if diag else tb
                s = (lax.dot_general(qm_ref[hh, r0:r0 + rb, :], k_ref[0:nk, :], NT, preferred_element_type=F32)
                     - c_ref[0, hh:hh + 1, 0:nk])
                if diag:
                    s = _causal_chunk(s, r0)
                _chunk_update(s, v_ref[0:nk, :], m_ref, l_ref, acc_ref, hh, r0, rb)

    pl.when(j < i)(lambda: step(False))

    @pl.when(j == i)
    def _():
        step(True)
        o_ref[...] = jnp.where(low, acc_ref[0] / l_ref[0], acc_ref[1] / l_ref[1])


def _fox_prompt(qf, foxb, ct, t, tb, rb):
    n = t // tb
    qi, kj = _tri_steps(n)
    npair = FOX_HEADS // 2
    gs = pltpu.PrefetchScalarGridSpec(
        num_scalar_prefetch=2, grid=(npair, qi.shape[0]),
        in_specs=[pl.BlockSpec((tb, LANES), lambda p, s, qi, kj: (qi[s], p)),
                  pl.BlockSpec((tb, LANES), lambda p, s, qi, kj: (kj[s], p)),
                  pl.BlockSpec((tb, LANES), lambda p, s, qi, kj: (kj[s], npair + p)),
                  pl.BlockSpec((1, 2, tb), lambda p, s, qi, kj: (p, 0, kj[s]))],
        out_specs=pl.BlockSpec((tb, LANES), lambda p, s, qi, kj: (qi[s], p)),
        scratch_shapes=[pltpu.VMEM((2, tb, LANES), BF), pltpu.VMEM((2, tb, LANES), F32),
                        pltpu.VMEM((2, tb, LANES), F32), pltpu.VMEM((2, tb, LANES), F32)])
    return pl.pallas_call(
        functools.partial(_foxp_body, tb=tb, rb=rb), grid_spec=gs,
        out_shape=jax.ShapeDtypeStruct((t, 512), F32),
        compiler_params=_cparams("parallel", "arbitrary"), name="fox_prompt")(qi, kj, qf, foxb, foxb, ct)


def _cmp_accumulate(rows_k, rows_v, wk_ref, wv_ref, abk_ref, abv_ref):
    acck = jnp.zeros(abk_ref.shape, F32)
    accv = jnp.zeros(abv_ref.shape, F32)
    for l in range(CMP_STRIDE):
        acck = acck + jnp.dot(rows_k(l).astype(BF), wk_ref[l], preferred_element_type=F32)
        accv = accv + jnp.dot(rows_v(l).astype(BF), wv_ref[l], preferred_element_type=F32)
    abk_ref[...] = acck
    abv_ref[...] = accv


def _cmpab_body(pt_ref, *refs, pps):
    del pt_ref
    kp, vp = refs[:pps], refs[pps:2 * pps]
    wk_ref, wv_ref, abk_ref, abv_ref = refs[2 * pps:]
    nsub = PAGE // CMP_STRIDE
    _cmp_accumulate(
        lambda l: jnp.concatenate([r[0, pl.ds(l, nsub, stride=CMP_STRIDE), :] for r in kp], axis=0),
        lambda l: jnp.concatenate([r[0, pl.ds(l, nsub, stride=CMP_STRIDE), :] for r in vp], axis=0),
        wk_ref, wv_ref, abk_ref, abv_ref)


def _cmpabt_body(pt_ref, *refs, pps):
    del pt_ref
    pages = refs[:pps]
    wk_ref, wv_ref, abk_ref, abv_ref, xk_ref, xv_ref = refs[pps:]
    for k, r in enumerate(pages):
        xk_ref[PAGE * k: PAGE * (k + 1), :] = r[0, 0].reshape(2 * HEAD_DIM, PAGE).T
        xv_ref[PAGE * k: PAGE * (k + 1), :] = r[0, 1].reshape(2 * HEAD_DIM, PAGE).T
    nsub = pps * PAGE // CMP_STRIDE
    _cmp_accumulate(lambda l: xk_ref[pl.ds(l, nsub, stride=CMP_STRIDE), :],
                    lambda l: xv_ref[pl.ds(l, nsub, stride=CMP_STRIDE), :], wk_ref, wv_ref, abk_ref, abv_ref)


def _cmp_ab_t(pool_t, pt_flat, wk, wv, pps):
    npg = pt_flat.shape[0]
    rows = pps * PAGE // CMP_STRIDE
    pspec = [pl.BlockSpec((1, 2, 2, HEAD_DIM, PAGE), lambda s, pt, k=k: (pt[s * pps + k], 0, 0, 0, 0)) for k in range(pps)]
    wspec = pl.BlockSpec((CMP_STRIDE, LANES, 512), lambda s, pt: (0, 0, 0), pipeline_mode=pl.Buffered(1))
    ospec = pl.BlockSpec((rows, 512), lambda s, pt: (s, 0))
    gs = pltpu.PrefetchScalarGridSpec(num_scalar_prefetch=1, grid=(npg // pps,), in_specs=pspec + [wspec, wspec],
                                      out_specs=[ospec, ospec],
                                      scratch_shapes=[pltpu.VMEM((pps * PAGE, LANES), F32), pltpu.VMEM((pps * PAGE, LANES), F32)])
    shp = jax.ShapeDtypeStruct((npg * PAGE // CMP_STRIDE, 512), F32)
    return pl.pallas_call(functools.partial(_cmpabt_body, pps=pps), grid_spec=gs, out_shape=[shp, shp],
                          compiler_params=_cparams("parallel"), name="cmp_ab_t")(pt_flat, *([pool_t] * pps), wk, wv)


def _cmp_ab(pool, pt_flat, wk, wv, pps):
    npg = pt_flat.shape[0]
    nsub = PAGE // CMP_STRIDE
    rows = pps * nsub
    kspec = [pl.BlockSpec((1, PAGE, LANES), lambda s, pt, k=k: (pt[s * pps + k], 0, 0)) for k in range(pps)]
    vspec = [pl.BlockSpec((1, PAGE, LANES), lambda s, pt, k=k: (pt[s * pps + k], 0, 1)) for k in range(pps)]
    wspec = pl.BlockSpec((CMP_STRIDE, LANES, 512), lambda s, pt: (0, 0, 0), pipeline_mode=pl.Buffered(1))
    ospec = pl.BlockSpec((rows, 512), lambda s, pt: (s, 0))
    gs = pltpu.PrefetchScalarGridSpec(num_scalar_prefetch=1, grid=(npg // pps,),
                                      in_specs=kspec + vspec + [wspec, wspec], out_specs=[ospec, ospec])
    shp = jax.ShapeDtypeStruct((npg * nsub, 512), F32)
    return pl.pallas_call(functools.partial(_cmpab_body, pps=pps), grid_spec=gs, out_shape=[shp, shp],
                          compiler_params=_cparams("parallel"), name="cmp_ab")(
        pt_flat, *([pool] * (2 * pps)), wk, wv)


def _gelu_tanh(x):
    return 0.5 * x * (1.0 + jnp.tanh(math.sqrt(2.0 / math.pi) * (x + 0.044715 * (x * x * x))))


def _cmpfin_body(abk_ref, abv_ref, pek_ref, w1k_ref, w2k_ref, pev_ref, w1v_ref, w2v_ref, seg_ref, gk_ref, c_ref, s_ref,
                 kc_ref, vc_ref, *, nc):
    lane = lax.broadcasted_iota(jnp.int32, (nc, LANES), 1)
    first8 = (lane % HEAD_DIM) < (ROT_DIM // 2)
    seg = seg_ref[...]
    cosv = c_ref[...]
    sinv = s_ref[...]

    def summaries(ab_ref, pe_ref, w1_ref, w2_ref, g):
        bias = jnp.dot(pe_ref[...].astype(BF), w1_ref[...].astype(BF), preferred_element_type=F32)[0:1, :]
        a = ab_ref[:, 256 * g: 256 * g + LANES]
        b = ab_ref[:, 256 * g + LANES: 256 * g + 2 * LANES]
        hid = _gelu_tanh(a + pltpu.roll(b, nc - 1, 0) + bias)
        return jnp.dot(hid.astype(BF), w2_ref[...], preferred_element_type=F32)

    for g in range(NSA_KV_HEADS):
        k = summaries(abk_ref, pek_ref, w1k_ref, w2k_ref, g)
        zz = k * k
        hi = zz.astype(BF)
        lo = (zz - hi.astype(F32)).astype(BF)
        ms = jnp.dot(hi, seg, preferred_element_type=F32) + jnp.dot(lo, seg, preferred_element_type=F32)
        k = k * lax.rsqrt(ms + EPS) * gk_ref[...]
        partner = jnp.where(first8, pltpu.roll(k, LANES - ROT_DIM // 2, 1), pltpu.roll(k, ROT_DIM // 2, 1))
        k = k * cosv + partner * sinv
        kc_ref[0, :, LANES * g: LANES * (g + 1)] = (k * SCALE).astype(BF)
        vc_ref[0, :, LANES * g: LANES * (g + 1)] = summaries(abv_ref, pev_ref, w1v_ref, w2v_ref, g).astype(BF)


def _cmp_fin(abk, abv, pek, w1k, w2k, pev, w1v, w2v, seg, gk, cosc, sinc, nseq, nc):
    ab = pl.BlockSpec((nc, 512), lambda b: (b, 0))
    kdim = CMP_BLOCK * HEAD_DIM
    out = pl.BlockSpec((1, nc, 256), lambda b: (b, 0, 0))
    shp = jax.ShapeDtypeStruct((nseq, nc, 256), BF)
    return pl.pallas_call(
        functools.partial(_cmpfin_body, nc=nc), grid=(nseq,),
        in_specs=[ab, ab, _const_spec((8, kdim)), _const_spec((kdim, CMP_HIDDEN)), _const_spec((CMP_HIDDEN, LANES)),
                  _const_spec((8, kdim)), _const_spec((kdim, CMP_HIDDEN)), _const_spec((CMP_HIDDEN, LANES)),
                  _const_spec((LANES, LANES)), _const_spec((1, LANES)), _const_spec((nc, LANES)), _const_spec((nc, LANES))],
        out_specs=[out, out], out_shape=[shp, shp],
        compiler_params=_cparams("parallel"), name="cmp_fin")(abk, abv, pek, w1k, w2k, pev, w1v, w2v, seg, gk, cosc, sinc)


def _cmpattn_body(seq_ref, pos_ref, q_ref, kc_ref, vc_ref, m_ref, oc_ref, sb_ref, *, nq, nc, nl, nblk, k_pick):
    del seq_ref
    pos0 = pos_ref[pl.program_id(0)]
    q = q_ref[0]
    lane = lax.broadcasted_iota(jnp.int32, (nq, LANES), 1)
    low = lane < HEAD_DIM
    trow = pos0 + lax.broadcasted_iota(jnp.int32, (nq, 1), 0)
    t4 = jnp.concatenate([trow] * 4, axis=0)
    posc = CMP_STRIDE * lax.broadcasted_iota(jnp.int32, (1, nc), 1) + (CMP_BLOCK - 1)
    valid = posc <= t4
    imps = []
    for g in range(NSA_KV_HEADS):
        rows = []
        for hh in range(4):
            h = 4 * g + hh
            pr = q[:, LANES * (h // 2): LANES * (h // 2 + 1)]
            rows.append(jnp.where(low, pr, 0.0) if h % 2 == 0 else jnp.where(low, 0.0, pr))
        qg = jnp.concatenate(rows, axis=0).astype(BF)
        s = lax.dot_general(qg, kc_ref[0, :, LANES * g: LANES * (g + 1)], NT, preferred_element_type=F32)
        s = jnp.where(valid, s, NEG)
        m = jnp.max(s, axis=1, keepdims=True)
        e = jnp.where(valid, jnp.exp(s - m), 0.0)
        dsum = jnp.sum(e, axis=1, keepdims=True)
        p = e / jnp.where(dsum > 0, dsum, 1.0)
        o = jnp.dot(p.astype(BF), vc_ref[0, :, LANES * g: LANES * (g + 1)], preferred_element_type=F32)
        for pj in range(2):
            oc_ref[0, :, 256 * g + LANES * pj: 256 * g + LANES * (pj + 1)] = jnp.where(
                low, o[2 * pj * nq: (2 * pj + 1) * nq], o[(2 * pj + 1) * nq: (2 * pj + 2) * nq])
        imps.append(p[0:nq] + p[nq:2 * nq] + p[2 * nq:3 * nq] + p[3 * nq:4 * nq])

    nr = max(2 * nq, LANES)
    imp = jnp.concatenate(imps + ([jnp.zeros((nr - 2 * nq, nc), F32)] if nr > 2 * nq else []), axis=0)
    mt = m_ref[...]
    ps = sum(lax.dot_general(mt, part, NT, preferred_element_type=F32) for part in _split3(imp))
    blk = lax.broadcasted_iota(jnp.int32, (nl, 1), 0)
    blkf = blk.astype(F32)
    tcol = pos0 + lax.broadcasted_iota(jnp.int32, (1, nr), 1) % nq
    cur = tcol // SEL_BLOCK
    force = (blk == 0) | (blk == cur) | (blk == cur - 1)
    score = jnp.where(force, FORCE_SCORE, jnp.where(blk * SEL_BLOCK <= tcol, ps, -1.0))
    score = jnp.where(blk < nblk, score, -1.0)

    def pick(_, sc):
        mx = jnp.max(sc, axis=0, keepdims=True)
        idx = jnp.min(jnp.where(sc == mx, blkf, 1e9), axis=0, keepdims=True)
        return jnp.where(blkf == idx, -2.0, sc)

    picked = lax.fori_loop(0, k_pick, pick, score) == -2.0
    sel_rows = jnp.where(picked, 0.0, SEL_OFF).T
    for g in range(NSA_KV_HEADS):
        sb_ref[0, :, nl * g: nl * (g + 1)] = sel_rows[nq * g: nq * (g + 1)].astype(BF)


def _cmp_attn(q3, kcd, vcd, mmat, seq_of, pos0, nq, nblk, k_pick):
    nb = q3.shape[0]
    nc = kcd.shape[1]
    nl = mmat.shape[0]
    gs = pltpu.PrefetchScalarGridSpec(
        num_scalar_prefetch=2, grid=(nb,),
        in_specs=[pl.BlockSpec((1, nq, 512), lambda b, sq, ps: (b, 0, 0)),
                  pl.BlockSpec((1, nc, 256), lambda b, sq, ps: (sq[b], 0, 0)),
                  pl.BlockSpec((1, nc, 256), lambda b, sq, ps: (sq[b], 0, 0)),
                  pl.BlockSpec((nl, nc), lambda b, sq, ps: (0, 0), pipeline_mode=pl.Buffered(1))],
        out_specs=[pl.BlockSpec((1, nq, 512), lambda b, sq, ps: (b, 0, 0)),
                   pl.BlockSpec((1, nq, 2 * nl), lambda b, sq, ps: (b, 0, 0))])
    return pl.pallas_call(
        functools.partial(_cmpattn_body, nq=nq, nc=nc, nl=nl, nblk=nblk, k_pick=k_pick), grid_spec=gs,
        out_shape=[jax.ShapeDtypeStruct((nb, nq, 512), F32), jax.ShapeDtypeStruct((nb, nq, 2 * nl), BF)],
        compiler_params=_cparams("parallel"), name="cmp_attn")(seq_of, pos0, q3, kcd, vcd, mmat)


def _group_rows(q, low, width_tail=None):
    rows = []
    for hh in range(4):
        pr = q[:, LANES * (hh // 2): LANES * (hh // 2 + 1)]
        rows.append(jnp.where(low, pr, 0.0) if hh % 2 == 0 else jnp.where(low, 0.0, pr))
    return rows


def _store_group(o_ref, o, low, n):
    for pj in range(2):
        o_ref[:, LANES * pj: LANES * (pj + 1)] = jnp.where(low, o[2 * pj * n: (2 * pj + 1) * n],
                                                           o[(2 * pj + 1) * n: (2 * pj + 2) * n])


def _selp_body(qi_ref, kj_ref, q_ref, sb_ref, ka_ref, v_ref, o_ref, qm_ref, m_ref, l_ref, acc_ref, *, tq, rb):
    sid = pl.program_id(1)
    i = qi_ref[sid]
    j = kj_ref[sid]
    low = lax.broadcasted_iota(jnp.int32, (tq, LANES), 1) < HEAD_DIM

    @pl.when(j == 0)
    def _():
        q = q_ref[...] * LOG2E
        for hh in range(4):
            pr = q[:, LANES * (hh // 2): LANES * (hh // 2 + 1)]
            sw = pltpu.roll(pr, HEAD_DIM, 1)
            qm_ref[hh] = (jnp.where(low, pr, sw) if hh % 2 == 0 else jnp.where(low, sw, pr)).astype(BF)
        m_ref[...] = jnp.full_like(m_ref, NEG)
        l_ref[...] = jnp.zeros_like(l_ref)
        acc_ref[...] = jnp.zeros_like(acc_ref)

    bias_half = (lax.broadcasted_iota(jnp.int32, (tq, LANES), 1) // HEAD_DIM) == ((j * tq) // (HEAD_DIM * SEL_BLOCK)) % 2

    def step(diag):
        for hh in range(4):
            for t0 in range(0, tq, rb):
                nk = t0 + rb if diag else tq
                qa = jnp.where(bias_half[t0:t0 + rb], sb_ref[t0:t0 + rb, :], qm_ref[hh, t0:t0 + rb, :])
                s = lax.dot_general(qa, ka_ref[0, 0:nk, :], NT, preferred_element_type=F32)
                if diag:
                    s = _causal_chunk(s, t0)
                _chunk_update(s, v_ref[0, 0:nk, :], m_ref, l_ref, acc_ref, 0, hh * tq + t0, rb)

    pl.when(j < i)(lambda: step(False))

    @pl.when(j == i)
    def _():
        step(True)
        _store_group(o_ref, acc_ref[0] / l_ref[0], low, tq)


def _sel_prompt(qn, selb, kaug, vdup, t, tq, rb, nl):
    n = t // tq
    qi, kj = _tri_steps(n)
    nsup = nl // LANES
    per_sup = LANES * SEL_BLOCK // tq
    gs = pltpu.PrefetchScalarGridSpec(
        num_scalar_prefetch=2, grid=(NSA_KV_HEADS, qi.shape[0]),
        in_specs=[pl.BlockSpec((tq, 256), lambda g, s, qi, kj: (qi[s], g)),
                  pl.BlockSpec((tq, LANES), lambda g, s, qi, kj: (qi[s], g * nsup + kj[s] // per_sup)),
                  pl.BlockSpec((1, tq, LANES), lambda g, s, qi, kj: (g, kj[s], 0)),
                  pl.BlockSpec((1, tq, LANES), lambda g, s, qi, kj: (g, kj[s], 0))],
        out_specs=pl.BlockSpec((tq, 256), lambda g, s, qi, kj: (qi[s], g)),
        scratch_shapes=[pltpu.VMEM((4, tq, LANES), BF), pltpu.VMEM((1, 4 * tq, LANES), F32),
                        pltpu.VMEM((1, 4 * tq, LANES), F32), pltpu.VMEM((1, 4 * tq, LANES), F32)])
    return pl.pallas_call(
        functools.partial(_selp_body, tq=tq, rb=rb), grid_spec=gs,
        out_shape=jax.ShapeDtypeStruct((t, 512), F32),
        compiler_params=_cparams("parallel", "arbitrary"), name="sel_prompt")(qi, kj, qn, selb, kaug, vdup)


def _winp_body(q_ref, *refs, tq, nprev):
    nt = nprev + 1
    k_refs, v_refs, o_ref = refs[:nt], refs[nt:2 * nt], refs[2 * nt]
    i = pl.program_id(1)
    low = lax.broadcasted_iota(jnp.int32, (tq, LANES), 1) < HEAD_DIM
    k = jnp.concatenate([r[0] for r in k_refs], axis=0)
    v = jnp.concatenate([r[0] for r in v_refs], axis=0)
    nk = k.shape[0]
    tpos = i * tq + lax.broadcasted_iota(jnp.int32, (tq, nk), 0)
    kpos = (i - nprev) * tq + lax.broadcasted_iota(jnp.int32, (tq, nk), 1)
    ok = (kpos <= tpos) & (tpos - kpos < WINDOW) & (kpos >= 0)
    outs = []
    for r in _group_rows(q_ref[...] * LOG2E, low):
        s = jnp.where(ok, lax.dot_general(r.astype(BF), k, NT, preferred_element_type=F32), NEG)
        e = jnp.exp2(s - jnp.max(s, axis=1, keepdims=True))
        outs.append(jnp.dot(e.astype(BF), v, preferred_element_type=F32) / jnp.sum(e, axis=1, keepdims=True))
    _store_group(o_ref, jnp.concatenate(outs, axis=0), low, tq)


def _win_prompt(qn, kwdup, vwdup, t, tq):
    nprev = WINDOW // tq
    kv = [pl.BlockSpec((1, tq, LANES), lambda g, i, dd=dd: (g, jnp.maximum(i - nprev + dd, 0), 0)) for dd in range(nprev + 1)]
    return pl.pallas_call(
        functools.partial(_winp_body, tq=tq, nprev=nprev), grid=(NSA_KV_HEADS, t // tq),
        in_specs=[pl.BlockSpec((tq, 256), lambda g, i: (i, g))] + kv + kv,
        out_specs=pl.BlockSpec((tq, 256), lambda g, i: (i, g)),
        out_shape=jax.ShapeDtypeStruct((t, 512), F32),
        compiler_params=_cparams("parallel", "parallel"), name="win_prompt")(
        qn, *([kwdup] * (nprev + 1)), *([vwdup] * (nprev + 1)))


ROWS_S = NSA_HEADS * NQS
ROWS_G = ROWS_S // NSA_KV_HEADS


def _new_key_mask(shape):
    r = lax.broadcasted_iota(jnp.int32, shape, 0) % NQS
    c = lax.broadcasted_iota(jnp.int32, shape, 1)
    return c <= r


def _slab_update(s, pv_of, m_ref, l_ref, acc_ref):
    m_prev = m_ref[0]
    m_new = jnp.maximum(m_prev, jnp.max(s, axis=1, keepdims=True))
    alpha = jnp.exp(m_prev - m_new)
    p = jnp.exp(s - m_new)
    l_ref[0] = alpha * l_ref[0] + jnp.sum(p, axis=1, keepdims=True)
    acc_ref[0] = alpha * acc_ref[0] + pv_of(p)
    m_ref[0] = m_new


def _slab_finish(sn, pv_of, m_ref, l_ref, acc_ref):
    m_prev = m_ref[0]
    m_new = jnp.maximum(m_prev, jnp.max(sn, axis=1, keepdims=True))
    alpha = jnp.exp(m_prev - m_new)
    pn = jnp.exp(sn - m_new)
    l = alpha * l_ref[0] + jnp.sum(pn, axis=1, keepdims=True)
    return (alpha * acc_ref[0] + pv_of(pn)) / l


def _slab_init(m_ref, l_ref, acc_ref):
    m_ref[...] = jnp.full_like(m_ref, NEG)
    l_ref[...] = jnp.zeros_like(l_ref)
    acc_ref[...] = jnp.zeros_like(acc_ref)


def _grp(a, g):
    return a[ROWS_G * g: ROWS_G * (g + 1)]


def _sels_body(pt_ref, q_ref, sb0_ref, sb1_ref, e_ref, kn_ref, vn_ref, *refs, pps, nsteps, per_chunk):
    del pt_ref
    pages = refs[:pps]
    o_ref, m_ref, l_ref, acc_ref = refs[pps:]
    sid = pl.program_id(1)
    pl.when(sid == 0)(lambda: _slab_init(m_ref, l_ref, acc_ref))

    q = q_ref[0] * SCALE
    qb = q.astype(BF)
    slab = lambda kind, g: jnp.concatenate([r[0, kind, g] for r in pages], axis=1).astype(BF)
    sbrows = jnp.concatenate([sb0_ref[0].astype(F32)] * 4 + [sb1_ref[0].astype(F32)] * 4, axis=0).astype(BF)
    bias = jnp.dot(sbrows, e_ref[sid % per_chunk], preferred_element_type=F32)
    s = jnp.concatenate([jnp.dot(_grp(qb, g), slab(0, g), preferred_element_type=F32)
                         for g in range(NSA_KV_HEADS)], axis=0) + bias
    _slab_update(s, lambda p: jnp.concatenate(
        [lax.dot_general(_grp(p, g).astype(BF), slab(1, g), NT, preferred_element_type=F32)
         for g in range(NSA_KV_HEADS)], axis=0), m_ref, l_ref, acc_ref)

    @pl.when(sid == nsteps - 1)
    def _():
        sn = jnp.concatenate([lax.dot_general(_grp(q, g), kn_ref[0, g], NT, preferred_element_type=F32)
                              for g in range(NSA_KV_HEADS)], axis=0)
        sn = jnp.where(_new_key_mask(sn.shape), sn, NEG)
        o_ref[0] = _slab_finish(sn, lambda pn: jnp.concatenate(
            [jnp.dot(_grp(pn, g), vn_ref[0, g], preferred_element_type=F32) for g in range(NSA_KV_HEADS)], axis=0),
            m_ref, l_ref, acc_ref)


def _sel_sample(qh, selb, kn, vn, pool_t, pt_flat, eexp, b, npg, pps, nl):
    nsteps = npg // pps
    per_chunk = eexp.shape[0]
    nchunk = nl // LANES
    pg = [pl.BlockSpec((1, 2, 2, HEAD_DIM, PAGE), lambda bb, s, pt, k=k: (pt[bb * npg + s * pps + k], 1, 0, 0, 0))
          for k in range(pps)]
    row = pl.BlockSpec((1, ROWS_S, HEAD_DIM), lambda bb, s, pt: (bb, 0, 0))
    new = pl.BlockSpec((1, NSA_KV_HEADS, NQS, HEAD_DIM), lambda bb, s, pt: (bb, 0, 0, 0))
    gs = pltpu.PrefetchScalarGridSpec(
        num_scalar_prefetch=1, grid=(b, nsteps),
        in_specs=[row,
                  pl.BlockSpec((1, NQS, LANES), lambda bb, s, pt: (bb, 0, s // per_chunk)),
                  pl.BlockSpec((1, NQS, LANES), lambda bb, s, pt: (bb, 0, nchunk + s // per_chunk)),
                  pl.BlockSpec(eexp.shape, lambda bb, s, pt: (0, 0, 0), pipeline_mode=pl.Buffered(1)),
                  new, new] + pg,
        out_specs=row,
        scratch_shapes=[pltpu.VMEM((1, ROWS_S, 1), F32), pltpu.VMEM((1, ROWS_S, 1), F32),
                        pltpu.VMEM((1, ROWS_S, HEAD_DIM), F32)])
    return pl.pallas_call(
        functools.partial(_sels_body, pps=pps, nsteps=nsteps, per_chunk=per_chunk), grid_spec=gs,
        out_shape=jax.ShapeDtypeStruct((b, ROWS_S, HEAD_DIM), F32),
        compiler_params=_cparams("parallel", "arbitrary"), name="sel_sample")(
        pt_flat, qh, selb, selb, eexp, kn, vn, *([pool_t] * pps))


def _wins_body(q_ref, st_ref, kn_ref, vn_ref, o_ref, *, wb):
    q = q_ref[0] * SCALE
    qb = q.astype(BF)
    s = jnp.concatenate([jnp.dot(_grp(qb, g), st_ref[0, 0, g].astype(BF), preferred_element_type=F32)
                         for g in range(NSA_KV_HEADS)], axis=0)
    tq = lax.broadcasted_iota(jnp.int32, s.shape, 0) % NQS
    n = lax.broadcasted_iota(jnp.int32, s.shape, 1)
    s = jnp.where(n > tq + (wb - WINDOW), s, NEG)
    sn = jnp.concatenate([lax.dot_general(_grp(q, g), kn_ref[0, g], NT, preferred_element_type=F32)
                          for g in range(NSA_KV_HEADS)], axis=0)
    sn = jnp.where(_new_key_mask(sn.shape), sn, NEG)
    m = jnp.maximum(jnp.max(s, axis=1, keepdims=True), jnp.max(sn, axis=1, keepdims=True))
    e = jnp.exp(s - m)
    en = jnp.exp(sn - m)
    den = jnp.sum(e, axis=1, keepdims=True) + jnp.sum(en, axis=1, keepdims=True)
    o = jnp.concatenate(
        [lax.dot_general(_grp(e, g).astype(BF), st_ref[0, 1, g].astype(BF), NT, preferred_element_type=F32)
         + jnp.dot(_grp(en, g), vn_ref[0, g], preferred_element_type=F32) for g in range(NSA_KV_HEADS)], axis=0)
    o_ref[0] = o / den


def _win_sample(qh, state_t, kn, vn, b, wb):
    row = pl.BlockSpec((1, ROWS_S, HEAD_DIM), lambda bb: (bb, 0, 0))
    new = pl.BlockSpec((1, NSA_KV_HEADS, NQS, HEAD_DIM), lambda bb: (bb, 0, 0, 0))
    return pl.pallas_call(
        functools.partial(_wins_body, wb=wb), grid=(b,),
        in_specs=[row, pl.BlockSpec((1, 2, NSA_KV_HEADS, HEAD_DIM, wb), lambda bb: (bb, 0, 0, 0, 0)), new, new],
        out_specs=row, out_shape=jax.ShapeDtypeStruct((b, ROWS_S, HEAD_DIM), F32),
        compiler_params=_cparams("parallel"), name="win_sample")(qh, state_t, kn, vn)


def _foxs_body(pt_ref, q_ref, kn_ref, vn_ref, lfn_ref, u_ref, *refs, pps, nsteps):
    del pt_ref
    kv = refs[:pps]
    lfp = refs[pps:2 * pps]
    o_ref, m_ref, l_ref, acc_ref, carry_ref = refs[2 * pps:]
    sid = pl.program_id(1)

    @pl.when(sid == 0)
    def _():
        _slab_init(m_ref, l_ref, acc_ref)
        carry_ref[...] = jnp.zeros_like(carry_ref)

    q = q_ref[0] * SCALE
    qb = q.astype(BF)
    head = lambda a, h: a[NQS * h: NQS * (h + 1)]
    slab = lambda kind, h: jnp.concatenate([r[0, kind, h] for r in kv], axis=1).astype(BF)
    lft = jnp.concatenate([r[0] for r in lfp], axis=1)
    cum = _dot3(lft, u_ref[...]) + carry_ref[:, 0:1]
    keys = cum.shape[1]
    carry_ref[...] = jnp.broadcast_to(cum[:, keys - 1:], carry_ref.shape)
    s = jnp.concatenate([jnp.dot(head(qb, h), slab(0, h), preferred_element_type=F32)
                         - jnp.broadcast_to(cum[h:h + 1, :], (NQS, keys)) for h in range(FOX_HEADS)], axis=0)
    _slab_update(s, lambda p: jnp.concatenate(
        [lax.dot_general(head(p, h).astype(BF), slab(1, h), NT, preferred_element_type=F32)
         for h in range(FOX_HEADS)], axis=0), m_ref, l_ref, acc_ref)

    @pl.when(sid == nsteps - 1)
    def _():
        lfn = lfn_ref[0]
        col = lax.broadcasted_iota(jnp.int32, (FOX_HEADS, NQS), 1)
        cn = carry_ref[:, 0:NQS]
        for i in range(NQS):
            cn = cn + jnp.where(col >= i, lfn[:, i:i + 1], 0.0)
        sn = jnp.concatenate([lax.dot_general(head(q, h), kn_ref[0, h], NT, preferred_element_type=F32)
                              - jnp.broadcast_to(cn[h:h + 1, :], (NQS, NQS)) for h in range(FOX_HEADS)], axis=0)
        sn = jnp.where(_new_key_mask(sn.shape), sn, NEG)
        o_ref[0] = _slab_finish(sn, lambda pn: jnp.concatenate(
            [jnp.dot(head(pn, h), vn_ref[0, h], preferred_element_type=F32) for h in range(FOX_HEADS)], axis=0),
            m_ref, l_ref, acc_ref)


def _fox_sample(qh, kn, vn, lfn, pool_t, lf_t, pt_flat, b, npg, pps):
    nsteps = npg // pps
    keys = pps * PAGE
    u = jnp.asarray(np.triu(np.ones((keys, keys), np.float32)), BF)
    kvs = [pl.BlockSpec((1, 2, FOX_HEADS, HEAD_DIM, PAGE), lambda bb, s, pt, k=k: (pt[bb * npg + s * pps + k], 0, 0, 0, 0))
           for k in range(pps)]
    lfs = [pl.BlockSpec((1, FOX_HEADS, PAGE), lambda bb, s, pt, k=k: (pt[bb * npg + s * pps + k], 0, 0)) for k in range(pps)]
    row = pl.BlockSpec((1, ROWS_S, HEAD_DIM), lambda bb, s, pt: (bb, 0, 0))
    new = pl.BlockSpec((1, FOX_HEADS, NQS, HEAD_DIM), lambda bb, s, pt: (bb, 0, 0, 0))
    gs = pltpu.PrefetchScalarGridSpec(
        num_scalar_prefetch=1, grid=(b, nsteps),
        in_specs=[row, new, new, pl.BlockSpec((1, FOX_HEADS, NQS), lambda bb, s, pt: (bb, 0, 0)),
                  pl.BlockSpec((keys, keys), lambda bb, s, pt: (0, 0), pipeline_mode=pl.Buffered(1))] + kvs + lfs,
        out_specs=row,
        scratch_shapes=[pltpu.VMEM((1, ROWS_S, 1), F32), pltpu.VMEM((1, ROWS_S, 1), F32),
                        pltpu.VMEM((1, ROWS_S, HEAD_DIM), F32), pltpu.VMEM((FOX_HEADS, LANES), F32)])
    return pl.pallas_call(
        functools.partial(_foxs_body, pps=pps, nsteps=nsteps), grid_spec=gs,
        out_shape=jax.ShapeDtypeStruct((b, ROWS_S, HEAD_DIM), F32),
        compiler_params=_cparams("parallel", "arbitrary"), name="fox_sample")(
        pt_flat, qh, kn, vn, lfn, u, *([pool_t] * pps), *([lf_t] * pps))


def _rope_tables(pos):
    half = ROT_DIM // 2
    inv = ROPE_THETA ** (-(jnp.arange(half, dtype=F32) * 2.0 / ROT_DIM))
    ang = pos.astype(F32)[:, None] * inv[None, :]
    cos, sin = jnp.cos(ang), jnp.sin(ang)
    n = pos.shape[0]
    c64 = jnp.concatenate([cos, cos, jnp.ones((n, HEAD_DIM - ROT_DIM), F32)], axis=1)
    s64 = jnp.concatenate([-sin, sin, jnp.zeros((n, HEAD_DIM - ROT_DIM), F32)], axis=1)
    return jnp.tile(c64, (1, 2)), jnp.tile(s64, (1, 2))


def _importance_matrix(nc, nl):
    m = np.zeros((nl, nc), np.float32)
    per = SEL_BLOCK // CMP_STRIDE
    for n in range(nc - 1):
        m[n // per, n] += 1.0
        m[(n + 1) // per, n] += 1.0
    return jnp.asarray(m, BF)


def _gate_expand():
    e = np.zeros((3, LANES, 512), np.float32)
    for i in range(3):
        for h in range(NSA_HEADS):
            e[i, 3 * h + i, HEAD_DIM * h: HEAD_DIM * (h + 1)] = 1.0
    return jnp.asarray(e, BF)


def _block_expand(pps):
    per_step = pps * PAGE // SEL_BLOCK
    per_chunk = LANES // per_step
    e = np.zeros((per_chunk, LANES, pps * PAGE), np.float32)
    for r in range(per_chunk):
        for key in range(pps * PAGE):
            e[r, r * per_step + key // SEL_BLOCK, key] = 1.0
    return jnp.asarray(e, BF)


def _cmp_weights(w1):
    top, bot = w1[:CMP_STRIDE], w1[CMP_STRIDE:]
    wl = jnp.concatenate([top, bot], axis=2)
    z = jnp.zeros_like(wl)
    return jnp.concatenate([jnp.concatenate([wl, z], axis=2), jnp.concatenate([z, wl], axis=2)], axis=1).astype(BF)


def _pad_rows(a, rp):
    return jnp.pad(a, ((0, rp - a.shape[0]),) + ((0, 0),) * (a.ndim - 1))


def kernel(x_prompt, x_sample, cache_nsa_kv, cache_fox_kv, cache_fox_logf, state_nsa_win_kv, page_table, norm_ffn1, w1_gate, w1_up, w1_down, norm_mix, w_in, b_forget, nsa_q_norm, nsa_k_norm, cmp_pe_k, cmp_w1_k, cmp_w2_k, cmp_pe_v, cmp_w1_v, cmp_w2_v, fox_q_norm, fox_k_norm, w_branch_nsa, w_branch_fox, w_out, norm_ffn2, w2_gate, w2_up, w2_down):
    bp, t, d = x_prompt.shape
    b, ts, _ = x_sample.shape
    assert bp == 1 and norm_ffn1.shape[0] == 1 and ts <= 4
    npg = page_table.shape[1]
    past = npg * PAGE
    n_pool = cache_nsa_kv.shape[1]
    wb = state_nsa_win_kv.shape[2]
    tm = 256
    tq_fox = min(1024, t)
    tq_sel = min(1024, t)
    rb = 1024
    tq_win = 512
    nq_cmp = 128
    pps_cmp = math.gcd(math.gcd(t // PAGE, npg), 32)
    pps_sel, pps_fox = math.gcd(npg, 16), math.gcd(npg, 8)
    assert t % PAGE == 0 and t % tq_fox == 0 and t % tq_sel == 0 and t % tq_win == 0 and WINDOW % tq_win == 0
    assert (HEAD_DIM * SEL_BLOCK) % tq_sel == 0 and LANES % (2 * pps_sel) == 0

    xs = jnp.pad(x_sample, ((0, 0), (0, NQS - ts), (0, 0))).reshape(b * NQS, d)
    r = t + b * NQS
    rp = -(-r // tm) * tm
    x_all = _pad_rows(jnp.concatenate([x_prompt[0], xs], axis=0), rp)
    pos = jnp.concatenate([jnp.arange(t), past + (jnp.arange(b * NQS) % NQS), jnp.zeros((rp - r,), jnp.int32)])
    cosv, sinv = _rope_tables(pos)

    f = w1_gate.shape[2]
    fp = -(-f // LANES) * LANES
    padc = lambda w: jnp.pad(w.astype(BF), ((0, 0), (0, fp - f)))
    padr = lambda w: jnp.pad(w.astype(BF), ((0, fp - f), (0, 0)))
    wi = w_in[0]
    o2, o3, o4, o5 = 1280, 1304, 2840, 2848
    wa = jnp.concatenate([wi[:, :o2], wi[:, o3:o4], wi[:, o5:], wi[:, o2:o3], wi[:, o4:o5],
                          jnp.zeros((d, LANES - 32), F32)], axis=1).astype(BF)
    bm = jnp.zeros((1, LANES), F32).at[0, 24:32].set(b_forget[0])
    g2 = lambda g: jnp.tile(g, 2)[None, :]
    seg = jnp.asarray(np.kron(np.eye(2, dtype=np.float32), np.full((HEAD_DIM, HEAD_DIM), 1.0 / HEAD_DIM, np.float32)), BF)

    x1 = _ffn(x_all, norm_ffn1, padc(w1_gate[0]), padc(w1_up[0]), padr(w1_down[0]), tm)
    (qn, nsa_rows, win_rows, qf, fox_rows, gm, misc, foxb, kaug, vdup, kwdup, vwdup) = _proj(
        x1, norm_mix, wa, seg, g2(nsa_q_norm[0]), g2(nsa_k_norm[0, 1]), g2(nsa_k_norm[0, 2]), g2(fox_q_norm[0]),
        g2(fox_k_norm[0]), bm, cosv, sinv, tm)

    ct = _cumsum_t(misc, t, min(512, t)).reshape(FOX_HEADS // 2, 2, t)
    o_fox_p = _fox_prompt(qf, foxb, ct, t, tq_fox, rb)

    wk, wv = _cmp_weights(cmp_w1_k[0]), _cmp_weights(cmp_w1_v[0])
    kdim = CMP_BLOCK * HEAD_DIM
    pe8 = lambda pe: jnp.broadcast_to(pe.reshape(1, kdim), (8, kdim))
    w2d = lambda w2: jnp.concatenate([w2, w2], axis=1).astype(BF)
    gkc = g2(nsa_k_norm[0, 0])

    def compress(ab_fn, pool, pt_flat, nseq, nc):
        abk, abv = ab_fn(pool, pt_flat, wk, wv, pps_cmp)
        posc = jnp.arange(nc) * CMP_STRIDE + (CMP_BLOCK - 1)
        cosc, sinc = _rope_tables(posc)
        return _cmp_fin(abk, abv, pe8(cmp_pe_k[0]), cmp_w1_k[0].reshape(kdim, CMP_HIDDEN), w2d(cmp_w2_k[0]),
                        pe8(cmp_pe_v[0]), cmp_w1_v[0].reshape(kdim, CMP_HIDDEN), w2d(cmp_w2_v[0]), seg, gkc, cosc, sinc,
                        nseq, nc)

    nsub = PAGE // CMP_STRIDE
    tpg = t // PAGE
    kc_p, vc_p = compress(_cmp_ab, nsa_rows[:t].reshape(tpg, PAGE, 512), jnp.arange(tpg, dtype=jnp.int32), 1, tpg * nsub)
    nsa_t = jnp.transpose(cache_nsa_kv[0], (0, 2, 3, 4, 1))
    fox_t = jnp.transpose(cache_fox_kv[0], (0, 2, 3, 4, 1))
    lf_t = jnp.transpose(cache_fox_logf[0], (0, 2, 1))
    st_t = jnp.transpose(state_nsa_win_kv[0], (0, 2, 3, 4, 1))
    pt_flat = page_table.reshape(-1).astype(jnp.int32)
    kc_s, vc_s = compress(_cmp_ab_t, nsa_t, pt_flat, b, npg * nsub)

    nl_p = -(-(t // SEL_BLOCK) // LANES) * LANES
    nl_s = -(-(past // SEL_BLOCK) // LANES) * LANES
    nbp = t // nq_cmp
    oc_p, sb_p = _cmp_attn(qn[:t].reshape(nbp, nq_cmp, 512), kc_p, vc_p, _importance_matrix(tpg * nsub, nl_p),
                           jnp.zeros((nbp,), jnp.int32), jnp.arange(nbp, dtype=jnp.int32) * nq_cmp, nq_cmp,
                           t // SEL_BLOCK, min(N_SEL, t // SEL_BLOCK))
    qn_s = qn[t:t + b * NQS].reshape(b, NQS, 512)
    oc_s, sb_s = _cmp_attn(qn_s, kc_s, vc_s, _importance_matrix(npg * nsub, nl_s), jnp.arange(b, dtype=jnp.int32),
                           jnp.full((b,), past, jnp.int32), NQS, past // SEL_BLOCK,
                           min(N_SEL, past // SEL_BLOCK + 1) - 1)

    os_p = _sel_prompt(qn, sb_p.reshape(t, 2 * nl_p), kaug, vdup, t, tq_sel, rb, nl_p)
    ow_p = _win_prompt(qn, kwdup, vwdup, t, tq_win)

    new = lambda a: a[t:t + b * NQS]
    hm = lambda a, nh: a.reshape(b, NQS, nh, HEAD_DIM).transpose(0, 2, 1, 3)
    rows_of = lambda o: o.reshape(b, NSA_HEADS, NQS, HEAD_DIM).transpose(0, 2, 1, 3).reshape(b * NQS, 512)
    qn_h = hm(new(qn), NSA_HEADS).reshape(b, ROWS_S, HEAD_DIM)
    nsa_new, win_new_r, fox_new = new(nsa_rows), new(win_rows), new(fox_rows)
    os_s = _sel_sample(qn_h, sb_s, hm(nsa_new[:, 256:384], NSA_KV_HEADS), hm(nsa_new[:, 384:512], NSA_KV_HEADS), nsa_t,
                       pt_flat, _block_expand(pps_sel), b, npg, pps_sel, nl_s)
    ow_s = _win_sample(qn_h, st_t, hm(win_new_r[:, 0:128], NSA_KV_HEADS), hm(win_new_r[:, 128:256], NSA_KV_HEADS), b, wb)
    of_s = _fox_sample(hm(new(qf), FOX_HEADS).reshape(b, ROWS_S, HEAD_DIM), hm(fox_new[:, 0:512], FOX_HEADS),
                       hm(fox_new[:, 512:1024], FOX_HEADS), new(misc)[:, 24:32].reshape(b, NQS, FOX_HEADS).transpose(0, 2, 1),
                       fox_t, lf_t, pt_flat, b, npg, pps_fox)

    cat = lambda p, s: _pad_rows(jnp.concatenate([p, s.reshape(b * NQS, 512)], axis=0), rp)
    os_s, ow_s, of_s = rows_of(os_s), rows_of(ow_s), rows_of(of_s)
    x2 = _merge(x1, cat(oc_p.reshape(t, 512), oc_s), cat(os_p, os_s), cat(ow_p, ow_s), cat(o_fox_p, of_s), misc, gm,
                _gate_expand(), w_branch_nsa[0].astype(BF), w_branch_fox[0].astype(BF), w_out[0].astype(BF), tm)
    y = _ffn(x2, norm_ffn2, padc(w2_gate[0]), padc(w2_up[0]), padr(w2_down[0]), tm)

    smp = lambda a, *shape: a[t:t + b * NQS].reshape((b, NQS) + shape)[:, :ts]
    g, hd = NSA_KV_HEADS, HEAD_DIM
    logf = misc[:, 24:32]
    win_new = smp(win_rows, 2, g, hd)
    win_all_s = jnp.concatenate([state_nsa_win_kv[0], win_new], axis=1)
    keep_p = min(WINDOW, t)
    keep_s = min(WINDOW, wb + ts)
    return (y[:t][None], smp(y, d),
            nsa_rows[:t].reshape(1, 1, t, 4, g, hd), smp(nsa_rows, 4, g, hd)[None],
            fox_rows[:t].reshape(1, 1, t, 2, FOX_HEADS, hd), smp(fox_rows, 2, FOX_HEADS, hd)[None],
            logf[:t].reshape(1, 1, t, FOX_HEADS), smp(logf, FOX_HEADS)[None],
            win_rows[t - keep_p:t].reshape(1, 1, keep_p, 2, g, hd), win_all_s[:, wb + ts - keep_s:][None])
```

```python
import functools
import math

import numpy as np
import jax
import jax.numpy as jnp
from jax import lax
from jax.experimental import pallas as pl
from jax.experimental.pallas import tpu as pltpu

F32 = jnp.float32
BF = jnp.bfloat16

HEAD_DIM = 64
NSA_HEADS = 8
NSA_KV_HEADS = 2
FOX_HEADS = 8
ROT_DIM = 16
ROPE_THETA = 500000.0
CMP_BLOCK = 32
CMP_STRIDE = 16
CMP_HIDDEN = 128
SEL_BLOCK = 64
N_SEL = 16
WINDOW = 512
PAGE = 128
EPS = 1e-6
FORCE_SCORE = 1e6
SCALE = 0.125
LOG2E = 1.4426950408889634

LANES = 128
NQS = 8
NEG = -1e30
SEL_OFF = -float(2 ** 30)
VMEM_LIMIT = 56 * 1024 * 1024

NT = (((1,), (1,)), ((), ()))


def _cparams(*sem):
    return pltpu.CompilerParams(dimension_semantics=sem, vmem_limit_bytes=VMEM_LIMIT)


def _const_spec(shape):
    return pl.BlockSpec(shape, lambda *_: (0,) * len(shape), pipeline_mode=pl.Buffered(1))


def _split3(a):
    hi = a.astype(BF)
    r1 = a - hi.astype(F32)
    mid = r1.astype(BF)
    lo = (r1 - mid.astype(F32)).astype(BF)
    return hi, mid, lo


def _dot3(a, m):
    hi, mid, lo = _split3(a)
    return (jnp.dot(hi, m, preferred_element_type=F32) + jnp.dot(mid, m, preferred_element_type=F32)
            + jnp.dot(lo, m, preferred_element_type=F32))


def _rms_rows(x, gain):
    return x * lax.rsqrt(jnp.mean(x * x, axis=-1, keepdims=True) + EPS) * gain


def _ffn_body(x_ref, n_ref, wg_ref, wu_ref, wd_ref, o_ref):
    x = x_ref[...]
    h = _rms_rows(x, n_ref[...]).astype(BF)
    g = jnp.dot(h, wg_ref[...], preferred_element_type=F32)
    u = jnp.dot(h, wu_ref[...], preferred_element_type=F32)
    a = (g * jax.nn.sigmoid(g) * u).astype(BF)
    o_ref[...] = x + 0.5 * jnp.dot(a, wd_ref[...], preferred_element_type=F32)


def _ffn(x, n, wg, wu, wd, tm):
    rp, d = x.shape
    f = wg.shape[1]
    row = pl.BlockSpec((tm, d), lambda i: (i, 0))
    return pl.pallas_call(
        _ffn_body, grid=(rp // tm,),
        in_specs=[row, _const_spec((1, d)), _const_spec((d, f)), _const_spec((d, f)), _const_spec((f, d))],
        out_specs=row, out_shape=jax.ShapeDtypeStruct((rp, d), F32),
        compiler_params=_cparams("parallel"), name="ffn")(x, n, wg, wu, wd)


def _proj_body(x_ref, nm_ref, w_ref, seg_ref, gq_ref, gks_ref, gkw_ref, gfq_ref, gfk_ref, bm_ref, c_ref, s_ref,
               qn_ref, nsa_ref, win_ref, qf_ref, fox_ref, gm_ref, misc_ref, foxb_ref, kaug_ref, vdup_ref, kwdup_ref,
               vwdup_ref, *, tm, d):
    x = x_ref[...]
    h = _rms_rows(x, nm_ref[...]).astype(BF)
    z = jnp.dot(h, w_ref[...], preferred_element_type=F32)
    cosv = c_ref[...]
    sinv = s_ref[...]
    seg = seg_ref[...]
    lane = lax.broadcasted_iota(jnp.int32, (tm, LANES), 1)
    first8 = (lane % HEAD_DIM) < (ROT_DIM // 2)
    low = lane < HEAD_DIM

    def hnorm(v, gain):
        zz = v * v
        hi = zz.astype(BF)
        lo = (zz - hi.astype(F32)).astype(BF)
        ms = jnp.dot(hi, seg, preferred_element_type=F32) + jnp.dot(lo, seg, preferred_element_type=F32)
        return v * lax.rsqrt(ms + EPS) * gain

    def rope(v):
        partner = jnp.where(first8, pltpu.roll(v, LANES - ROT_DIM // 2, 1), pltpu.roll(v, ROT_DIM // 2, 1))
        return v * cosv + partner * sinv

    def dup(v):
        r = pltpu.roll(v, HEAD_DIM, 1)
        return jnp.where(low, v, r), jnp.where(low, r, v)

    def chunk(off, c):
        return z[:, off + LANES * c: off + LANES * (c + 1)]

    o_q, o_kv, o_f, o_gm = 0, 512, 1280, 2816
    o_misc = o_gm + 2 * d
    for c in range(4):
        qn_ref[:, LANES * c: LANES * (c + 1)] = rope(hnorm(chunk(o_q, c), gq_ref[...]))
        qf_ref[:, LANES * c: LANES * (c + 1)] = hnorm(chunk(o_f, c), gfq_ref[...])
        fk = hnorm(chunk(o_f + 512, c), gfk_ref[...])
        fox_ref[:, LANES * c: LANES * (c + 1)] = fk
        foxb_ref[:, LANES * c: LANES * (c + 1)] = fk.astype(BF)
    fv = z[:, o_f + 1024: o_f + 1536]
    fox_ref[:, 512:1024] = fv
    foxb_ref[:, 512:1024] = fv.astype(BF)
    nsa_ref[:, 0:256] = z[:, o_kv: o_kv + 256]
    ks = rope(hnorm(chunk(o_kv, 2), gks_ref[...]))
    vs = chunk(o_kv, 3)
    kw = rope(hnorm(chunk(o_kv, 4), gkw_ref[...]))
    vw = chunk(o_kv, 5)
    nsa_ref[:, 256:384] = ks
    nsa_ref[:, 384:512] = vs
    win_ref[:, 0:128] = kw
    win_ref[:, 128:256] = vw
    gm_ref[...] = jax.nn.sigmoid(z[:, o_gm: o_gm + 2 * d])
    zm = z[:, o_misc: o_misc + LANES] + bm_ref[...]
    logsig = jnp.minimum(zm, 0.0) - jnp.log1p(jnp.exp(-jnp.abs(zm)))
    misc_ref[...] = jnp.where(lane < 3 * NSA_HEADS, jax.nn.sigmoid(zm), logsig)
    pos = pl.program_id(0) * tm + lax.broadcasted_iota(jnp.int32, (tm, 1), 0)
    blk = pos // SEL_BLOCK
    onehot = jnp.where(blk % LANES == lane, 1.0, 0.0)
    hot_half = (lane // HEAD_DIM) == (blk // HEAD_DIM) % 2
    k0, k1 = dup(ks * SCALE)
    v0, v1 = dup(vs)
    kw0, kw1 = dup(kw * SCALE)
    vw0, vw1 = dup(vw)
    for g, (kk, vv, kkw, vvw) in enumerate(((k0, v0, kw0, vw0), (k1, v1, kw1, vw1))):
        kaug_ref[g] = jnp.where(hot_half, onehot, kk).astype(BF)
        vdup_ref[g] = vv.astype(BF)
        kwdup_ref[g] = kkw.astype(BF)
        vwdup_ref[g] = vvw.astype(BF)


def _proj(x, nm, wa, seg, gq, gks, gkw, gfq, gfk, bm, cosv, sinv, tm):
    rp, d = x.shape
    n = wa.shape[1]
    row = lambda w: pl.BlockSpec((tm, w), lambda i: (i, 0))
    grp = lambda w: pl.BlockSpec((2, tm, w), lambda i: (0, i, 0))
    g128 = _const_spec((1, LANES))
    outs = [(512, F32), (512, F32), (256, F32), (512, F32), (1024, F32), (2 * d, F32), (LANES, F32), (1024, BF)]
    out_shape = [jax.ShapeDtypeStruct((rp, w), dt) for w, dt in outs]
    out_shape += [jax.ShapeDtypeStruct((2, rp, LANES), BF) for _ in range(4)]
    out_specs = [row(w) for w, _ in outs] + [grp(LANES) for _ in range(4)]
    return pl.pallas_call(
        functools.partial(_proj_body, tm=tm, d=d), grid=(rp // tm,),
        in_specs=[row(d), _const_spec((1, d)), _const_spec((d, n)), _const_spec((LANES, LANES)),
                  g128, g128, g128, g128, g128, g128, row(LANES), row(LANES)],
        out_specs=out_specs, out_shape=out_shape,
        compiler_params=_cparams("parallel"), name="proj")(x, nm, wa, seg, gq, gks, gkw, gfq, gfk, bm, cosv, sinv)


def _merge_body(x_ref, oc_ref, os_ref, ow_ref, of_ref, misc_ref, gm_ref, e_ref, wbn_ref, wbf_ref, wo_ref, o_ref, *, d):
    hi, mid, lo = _split3(misc_ref[...])

    def gate(i):
        e = e_ref[i]
        return (jnp.dot(hi, e, preferred_element_type=F32) + jnp.dot(mid, e, preferred_element_type=F32)
                + jnp.dot(lo, e, preferred_element_type=F32))

    o_nsa = gate(0) * oc_ref[...] + gate(1) * os_ref[...] + gate(2) * ow_ref[...]
    a = jnp.dot(o_nsa.astype(BF), wbn_ref[...], preferred_element_type=F32)
    b = jnp.dot(of_ref[...].astype(BF), wbf_ref[...], preferred_element_type=F32)
    gm = gm_ref[...]
    mixed = gm[:, :d] * a + gm[:, d:] * b
    o_ref[...] = x_ref[...] + jnp.dot(mixed.astype(BF), wo_ref[...], preferred_element_type=F32)


def _merge(x, oc, os_, ow, of, misc, gm, e, wbn, wbf, wo, tm):
    rp, d = x.shape
    row = lambda w: pl.BlockSpec((tm, w), lambda i: (i, 0))
    return pl.pallas_call(
        functools.partial(_merge_body, d=d), grid=(rp // tm,),
        in_specs=[row(d), row(512), row(512), row(512), row(512), row(LANES), row(2 * d),
                  _const_spec((3, LANES, 512)), _const_spec((512, d)), _const_spec((512, d)), _const_spec((d, d))],
        out_specs=row(d), out_shape=jax.ShapeDtypeStruct((rp, d), F32),
        compiler_params=_cparams("parallel"), name="merge")(x, oc, os_, ow, of, misc, gm, e, wbn, wbf, wo)


def _cumsum_body(m_ref, u_ref, o_ref, carry_ref, *, tb):
    @pl.when(pl.program_id(0) == 0)
    def _():
        carry_ref[...] = jnp.zeros_like(carry_ref)

    lft = m_ref[...].T
    a = lft[3 * NSA_HEADS: 3 * NSA_HEADS + FOX_HEADS, :]
    c = _dot3(a, u_ref[...]) + carry_ref[:, 0:1]
    o_ref[...] = c * LOG2E
    carry_ref[...] = jnp.broadcast_to(c[:, tb - 1: tb], carry_ref.shape)


def _cumsum_t(misc, t, tb):
    u = jnp.asarray(np.triu(np.ones((tb, tb), np.float32)), BF)
    return pl.pallas_call(
        functools.partial(_cumsum_body, tb=tb), grid=(t // tb,),
        in_specs=[pl.BlockSpec((tb, LANES), lambda i: (i, 0)), _const_spec((tb, tb))],
        out_specs=pl.BlockSpec((FOX_HEADS, tb), lambda i: (0, i)),
        out_shape=jax.ShapeDtypeStruct((FOX_HEADS, t), F32),
        scratch_shapes=[pltpu.VMEM((FOX_HEADS, LANES), F32)],
        compiler_params=_cparams("arbitrary"), name="fox_cumsum")(misc, u)


def _tri_steps(n):
    qi = np.concatenate([np.full(i + 1, i, np.int32) for i in range(n)])
    kj = np.concatenate([np.arange(i + 1, dtype=np.int32) for i in range(n)])
    return jnp.asarray(qi), jnp.asarray(kj)


def _chunk_update(s, v, m_ref, l_ref, acc_ref, idx, r0, rb):
    rows = slice(r0, r0 + rb)
    m_prev = m_ref[idx, rows, :]
    m_new = jnp.maximum(m_prev, jnp.max(s, axis=1, keepdims=True))
    alpha = jnp.exp2(m_prev - m_new)
    p = jnp.exp2(s - jnp.tile(m_new, (1, s.shape[1] // LANES)))
    l_ref[idx, rows, :] = alpha * l_ref[idx, rows, :] + jnp.sum(p, axis=1, keepdims=True)
    acc_ref[idx, rows, :] = alpha * acc_ref[idx, rows, :] + jnp.dot(p.astype(BF), v, preferred_element_type=F32)
    m_ref[idx, rows, :] = m_new


def _causal_chunk(s, r0):
    r = r0 + lax.broadcasted_iota(jnp.int32, s.shape, 0)
    c = lax.broadcasted_iota(jnp.int32, s.shape, 1)
    return jnp.where(c <= r, s, NEG)


def _foxp_body(qi_ref, kj_ref, q_ref, k_ref, v_ref, c_ref, o_ref, qm_ref, m_ref, l_ref, acc_ref, *, tb, rb):
    sid = pl.program_id(1)
    i = qi_ref[sid]
    j = kj_ref[sid]
    low = lax.broadcasted_iota(jnp.int32, (tb, LANES), 1) < HEAD_DIM

    @pl.when(j == 0)
    def _():
        q = q_ref[...] * (SCALE * LOG2E)
        qm_ref[0] = jnp.where(low, q, 0.0).astype(BF)
        qm_ref[1] = jnp.where(low, 0.0, q).astype(BF)
        m_ref[...] = jnp.full_like(m_ref, NEG)
        l_ref[...] = jnp.zeros_like(l_ref)
        acc_ref[...] = jnp.zeros_like(acc_ref)

    def step(diag):
        for hh in range(2):
            for r0 in range(0, tb, rb):
                nk = r0 + rb if diag else tb
                s = (lax.dot_general(qm_ref[hh, r0:r0 + rb, :], k_ref[0:nk, :], NT, preferred_element_type=F32)
                     - c_ref[0, hh:hh + 1, 0:nk])
                if diag:
                    s = _causal_chunk(s, r0)
                _chunk_update(s, v_ref[0:nk, :], m_ref, l_ref, acc_ref, hh, r0, rb)

    pl.when(j < i)(lambda: step(False))

    @pl.when(j == i)
    def _():
        step(True)
        o_ref[...] = jnp.where(low, acc_ref[0] / l_ref[0], acc_ref[1] / l_ref[1])


def _fox_prompt(qf, foxb, ct, t, tb, rb):
    n = t // tb
    qi, kj = _tri_steps(n)
    npair = FOX_HEADS // 2
    gs = pltpu.PrefetchScalarGridSpec(
        num_scalar_prefetch=2, grid=(npair, qi.shape[0]),
        in_specs=[pl.BlockSpec((tb, LANES), lambda p, s, qi, kj: (qi[s], p)),
                  pl.BlockSpec((tb, LANES), lambda p, s, qi, kj: (kj[s], p)),
                  pl.BlockSpec((tb, LANES), lambda p, s, qi, kj: (kj[s], npair + p)),
                  pl.BlockSpec((1, 2, tb), lambda p, s, qi, kj: (p, 0, kj[s]))],
        out_specs=pl.BlockSpec((tb, LANES), lambda p, s, qi, kj: (qi[s], p)),
        scratch_shapes=[pltpu.VMEM((2, tb, LANES), BF), pltpu.VMEM((2, tb, LANES), F32),
                        pltpu.VMEM((2, tb, LANES), F32), pltpu.VMEM((2, tb, LANES), F32)])
    return pl.pallas_call(
        functools.partial(_foxp_body, tb=tb, rb=rb), grid_spec=gs,
        out_shape=jax.ShapeDtypeStruct((t, 512), F32),
        compiler_params=_cparams("parallel", "arbitrary"), name="fox_prompt")(qi, kj, qf, foxb, foxb, ct)


def _cmp_accumulate(rows_k, rows_v, wk_ref, wv_ref, abk_ref, abv_ref):
    xk = jnp.concatenate([rows_k(l) for l in range(CMP_STRIDE)], axis=1).astype(BF)
    xv = jnp.concatenate([rows_v(l) for l in range(CMP_STRIDE)], axis=1).astype(BF)
    abk_ref[...] = jnp.dot(xk, wk_ref[...], preferred_element_type=F32)
    abv_ref[...] = jnp.dot(xv, wv_ref[...], preferred_element_type=F32)


def _cmpab_body(pt_ref, *refs, pps):
    del pt_ref
    kp, vp = refs[:pps], refs[pps:2 * pps]
    wk_ref, wv_ref, abk_ref, abv_ref = refs[2 * pps:]
    nsub = PAGE // CMP_STRIDE
    _cmp_accumulate(
        lambda l: jnp.concatenate([r[0, pl.ds(l, nsub, stride=CMP_STRIDE), :] for r in kp], axis=0),
        lambda l: jnp.concatenate([r[0, pl.ds(l, nsub, stride=CMP_STRIDE), :] for r in vp], axis=0),
        wk_ref, wv_ref, abk_ref, abv_ref)


def _cmpabt_body(pt_ref, *refs, pps):
    del pt_ref
    pages = refs[:pps]
    wk_ref, wv_ref, abk_ref, abv_ref, xk_ref, xv_ref = refs[pps:]
    for k, r in enumerate(pages):
        xk_ref[PAGE * k: PAGE * (k + 1), :] = r[0, 0].reshape(2 * HEAD_DIM, PAGE).T
        xv_ref[PAGE * k: PAGE * (k + 1), :] = r[0, 1].reshape(2 * HEAD_DIM, PAGE).T
    nsub = pps * PAGE // CMP_STRIDE
    _cmp_accumulate(lambda l: xk_ref[pl.ds(l, nsub, stride=CMP_STRIDE), :],
                    lambda l: xv_ref[pl.ds(l, nsub, stride=CMP_STRIDE), :], wk_ref, wv_ref, abk_ref, abv_ref)


def _cmp_ab_t(pool_t, pt_flat, wk, wv, pps):
    npg = pt_flat.shape[0]
    rows = pps * PAGE // CMP_STRIDE
    pspec = [pl.BlockSpec((1, 2, 2, HEAD_DIM, PAGE), lambda s, pt, k=k: (pt[s * pps + k], 0, 0, 0, 0)) for k in range(pps)]
    wspec = pl.BlockSpec((CMP_STRIDE * LANES, 512), lambda s, pt: (0, 0), pipeline_mode=pl.Buffered(1))
    ospec = pl.BlockSpec((rows, 512), lambda s, pt: (s, 0))
    gs = pltpu.PrefetchScalarGridSpec(num_scalar_prefetch=1, grid=(npg // pps,), in_specs=pspec + [wspec, wspec],
                                      out_specs=[ospec, ospec],
                                      scratch_shapes=[pltpu.VMEM((pps * PAGE, LANES), F32), pltpu.VMEM((pps * PAGE, LANES), F32)])
    shp = jax.ShapeDtypeStruct((npg * PAGE // CMP_STRIDE, 512), F32)
    return pl.pallas_call(functools.partial(_cmpabt_body, pps=pps), grid_spec=gs, out_shape=[shp, shp],
                          compiler_params=_cparams("parallel"), name="cmp_ab_t")(pt_flat, *([pool_t] * pps), wk, wv)


def _cmp_ab(pool, pt_flat, wk, wv, pps):
    npg = pt_flat.shape[0]
    nsub = PAGE // CMP_STRIDE
    rows = pps * nsub
    kspec = [pl.BlockSpec((1, PAGE, LANES), lambda s, pt, k=k: (pt[s * pps + k], 0, 0)) for k in range(pps)]
    vspec = [pl.BlockSpec((1, PAGE, LANES), lambda s, pt, k=k: (pt[s * pps + k], 0, 1)) for k in range(pps)]
    wspec = pl.BlockSpec((CMP_STRIDE * LANES, 512), lambda s, pt: (0, 0), pipeline_mode=pl.Buffered(1))
    ospec = pl.BlockSpec((rows, 512), lambda s, pt: (s, 0))
    gs = pltpu.PrefetchScalarGridSpec(num_scalar_prefetch=1, grid=(npg // pps,),
                                      in_specs=kspec + vspec + [wspec, wspec], out_specs=[ospec, ospec])
    shp = jax.ShapeDtypeStruct((npg * nsub, 512), F32)
    return pl.pallas_call(functools.partial(_cmpab_body, pps=pps), grid_spec=gs, out_shape=[shp, shp],
                          compiler_params=_cparams("parallel"), name="cmp_ab")(
        pt_flat, *([pool] * (2 * pps)), wk, wv)


def _gelu_tanh(x):
    return 0.5 * x * (1.0 + jnp.tanh(math.sqrt(2.0 / math.pi) * (x + 0.044715 * (x * x * x))))


def _cmpfin_body(abk_ref, abv_ref, pek_ref, w1k_ref, w2k_ref, pev_ref, w1v_ref, w2v_ref, seg_ref, gk_ref, c_ref, s_ref,
                 kc_ref, vc_ref, *, nc):
    lane = lax.broadcasted_iota(jnp.int32, (nc, LANES), 1)
    first8 = (lane % HEAD_DIM) < (ROT_DIM // 2)
    seg = seg_ref[...]
    cosv = c_ref[...]
    sinv = s_ref[...]

    def summaries(ab_ref, pe_ref, w1_ref, w2_ref, g):
        bias = jnp.dot(pe_ref[...].astype(BF), w1_ref[...].astype(BF), preferred_element_type=F32)[0:1, :]
        a = ab_ref[:, 256 * g: 256 * g + LANES]
        b = ab_ref[:, 256 * g + LANES: 256 * g + 2 * LANES]
        hid = _gelu_tanh(a + pltpu.roll(b, nc - 1, 0) + bias)
        return jnp.dot(hid.astype(BF), w2_ref[...], preferred_element_type=F32)

    for g in range(NSA_KV_HEADS):
        k = summaries(abk_ref, pek_ref, w1k_ref, w2k_ref, g)
        zz = k * k
        hi = zz.astype(BF)
        lo = (zz - hi.astype(F32)).astype(BF)
        ms = jnp.dot(hi, seg, preferred_element_type=F32) + jnp.dot(lo, seg, preferred_element_type=F32)
        k = k * lax.rsqrt(ms + EPS) * gk_ref[...]
        partner = jnp.where(first8, pltpu.roll(k, LANES - ROT_DIM // 2, 1), pltpu.roll(k, ROT_DIM // 2, 1))
        k = k * cosv + partner * sinv
        kc_ref[0, :, LANES * g: LANES * (g + 1)] = (k * SCALE).astype(BF)
        vc_ref[0, :, LANES * g: LANES * (g + 1)] = summaries(abv_ref, pev_ref, w1v_ref, w2v_ref, g).astype(BF)


def _cmp_fin(abk, abv, pek, w1k, w2k, pev, w1v, w2v, seg, gk, cosc, sinc, nseq, nc):
    ab = pl.BlockSpec((nc, 512), lambda b: (b, 0))
    kdim = CMP_BLOCK * HEAD_DIM
    out = pl.BlockSpec((1, nc, 256), lambda b: (b, 0, 0))
    shp = jax.ShapeDtypeStruct((nseq, nc, 256), BF)
    return pl.pallas_call(
        functools.partial(_cmpfin_body, nc=nc), grid=(nseq,),
        in_specs=[ab, ab, _const_spec((8, kdim)), _const_spec((kdim, CMP_HIDDEN)), _const_spec((CMP_HIDDEN, LANES)),
                  _const_spec((8, kdim)), _const_spec((kdim, CMP_HIDDEN)), _const_spec((CMP_HIDDEN, LANES)),
                  _const_spec((LANES, LANES)), _const_spec((1, LANES)), _const_spec((nc, LANES)), _const_spec((nc, LANES))],
        out_specs=[out, out], out_shape=[shp, shp],
        compiler_params=_cparams("parallel"), name="cmp_fin")(abk, abv, pek, w1k, w2k, pev, w1v, w2v, seg, gk, cosc, sinc)


def _cmpattn_body(seq_ref, pos_ref, q_ref, kc_ref, vc_ref, m_ref, oc_ref, sb_ref, *, nq, nc, nl, nblk, k_pick):
    del seq_ref
    pos0 = pos_ref[pl.program_id(0)]
    q = q_ref[0]
    lane = lax.broadcasted_iota(jnp.int32, (nq, LANES), 1)
    low = lane < HEAD_DIM
    trow = pos0 + lax.broadcasted_iota(jnp.int32, (nq, 1), 0)
    t4 = jnp.concatenate([trow] * 4, axis=0)
    posc = CMP_STRIDE * lax.broadcasted_iota(jnp.int32, (1, nc), 1) + (CMP_BLOCK - 1)
    valid = posc <= t4
    imps = []
    for g in range(NSA_KV_HEADS):
        rows = []
        for hh in range(4):
            h = 4 * g + hh
            pr = q[:, LANES * (h // 2): LANES * (h // 2 + 1)]
            rows.append(jnp.where(low, pr, 0.0) if h % 2 == 0 else jnp.where(low, 0.0, pr))
        qg = jnp.concatenate(rows, axis=0).astype(BF)
        s = lax.dot_general(qg, kc_ref[0, :, LANES * g: LANES * (g + 1)], NT, preferred_element_type=F32)
        s = jnp.where(valid, s, NEG)
        m = jnp.max(s, axis=1, keepdims=True)
        e = jnp.where(valid, jnp.exp(s - m), 0.0)
        dsum = jnp.sum(e, axis=1, keepdims=True)
        p = e / jnp.where(dsum > 0, dsum, 1.0)
        o = jnp.dot(p.astype(BF), vc_ref[0, :, LANES * g: LANES * (g + 1)], preferred_element_type=F32)
        for pj in range(2):
            oc_ref[0, :, 256 * g + LANES * pj: 256 * g + LANES * (pj + 1)] = jnp.where(
                low, o[2 * pj * nq: (2 * pj + 1) * nq], o[(2 * pj + 1) * nq: (2 * pj + 2) * nq])
        imps.append(p[0:nq] + p[nq:2 * nq] + p[2 * nq:3 * nq] + p[3 * nq:4 * nq])

    nr = max(2 * nq, LANES)
    imp = jnp.concatenate(imps + ([jnp.zeros((nr - 2 * nq, nc), F32)] if nr > 2 * nq else []), axis=0)
    mt = m_ref[...]
    ps = sum(lax.dot_general(mt, part, NT, preferred_element_type=F32) for part in _split3(imp))
    blk = lax.broadcasted_iota(jnp.int32, (nl, 1), 0)
    blkf = blk.astype(F32)
    tcol = pos0 + lax.broadcasted_iota(jnp.int32, (1, nr), 1) % nq
    cur = tcol // SEL_BLOCK
    force = (blk == 0) | (blk == cur) | (blk == cur - 1)
    score = jnp.where(force, FORCE_SCORE, jnp.where(blk * SEL_BLOCK <= tcol, ps, -1.0))
    score = jnp.where(blk < nblk, score, -1.0)

    def pick(_, sc):
        mx = jnp.max(sc, axis=0, keepdims=True)
        idx = jnp.min(jnp.where(sc == mx, blkf, 1e9), axis=0, keepdims=True)
        return jnp.where(blkf == idx, -2.0, sc)

    picked = lax.fori_loop(0, k_pick, pick, score) == -2.0
    sel_rows = jnp.where(picked, 0.0, SEL_OFF).T
    for g in range(NSA_KV_HEADS):
        sb_ref[0, :, nl * g: nl * (g + 1)] = sel_rows[nq * g: nq * (g + 1)].astype(BF)


def _cmp_attn(q3, kcd, vcd, mmat, seq_of, pos0, nq, nblk, k_pick):
    nb = q3.shape[0]
    nc = kcd.shape[1]
    nl = mmat.shape[0]
    gs = pltpu.PrefetchScalarGridSpec(
        num_scalar_prefetch=2, grid=(nb,),
        in_specs=[pl.BlockSpec((1, nq, 512), lambda b, sq, ps: (b, 0, 0)),
                  pl.BlockSpec((1, nc, 256), lambda b, sq, ps: (sq[b], 0, 0)),
                  pl.BlockSpec((1, nc, 256), lambda b, sq, ps: (sq[b], 0, 0)),
                  pl.BlockSpec((nl, nc), lambda b, sq, ps: (0, 0), pipeline_mode=pl.Buffered(1))],
        out_specs=[pl.BlockSpec((1, nq, 512), lambda b, sq, ps: (b, 0, 0)),
                   pl.BlockSpec((1, nq, 2 * nl), lambda b, sq, ps: (b, 0, 0))])
    return pl.pallas_call(
        functools.partial(_cmpattn_body, nq=nq, nc=nc, nl=nl, nblk=nblk, k_pick=k_pick), grid_spec=gs,
        out_shape=[jax.ShapeDtypeStruct((nb, nq, 512), F32), jax.ShapeDtypeStruct((nb, nq, 2 * nl), BF)],
        compiler_params=_cparams("parallel"), name="cmp_attn")(seq_of, pos0, q3, kcd, vcd, mmat)


def _group_rows(q, low, width_tail=None):
    rows = []
    for hh in range(4):
        pr = q[:, LANES * (hh // 2): LANES * (hh // 2 + 1)]
        rows.append(jnp.where(low, pr, 0.0) if hh % 2 == 0 else jnp.where(low, 0.0, pr))
    return rows


def _store_group(o_ref, o, low, n):
    for pj in range(2):
        o_ref[:, LANES * pj: LANES * (pj + 1)] = jnp.where(low, o[2 * pj * n: (2 * pj + 1) * n],
                                                           o[(2 * pj + 1) * n: (2 * pj + 2) * n])


def _selp_body(qi_ref, kj_ref, q_ref, sb_ref, ka_ref, v_ref, o_ref, qm_ref, m_ref, l_ref, acc_ref, *, tq, rb):
    sid = pl.program_id(1)
    i = qi_ref[sid]
    j = kj_ref[sid]
    low = lax.broadcasted_iota(jnp.int32, (tq, LANES), 1) < HEAD_DIM

    @pl.when(j == 0)
    def _():
        q = q_ref[...] * LOG2E
        for hh in range(4):
            pr = q[:, LANES * (hh // 2): LANES * (hh // 2 + 1)]
            sw = pltpu.roll(pr, HEAD_DIM, 1)
            qm_ref[hh] = (jnp.where(low, pr, sw) if hh % 2 == 0 else jnp.where(low, sw, pr)).astype(BF)
        m_ref[...] = jnp.full_like(m_ref, NEG)
        l_ref[...] = jnp.zeros_like(l_ref)
        acc_ref[...] = jnp.zeros_like(acc_ref)

    bias_half = (lax.broadcasted_iota(jnp.int32, (tq, LANES), 1) // HEAD_DIM) == ((j * tq) // (HEAD_DIM * SEL_BLOCK)) % 2

    def step(diag):
        for hh in range(4):
            for t0 in range(0, tq, rb):
                nk = t0 + rb if diag else tq
                qa = jnp.where(bias_half[t0:t0 + rb], sb_ref[t0:t0 + rb, :], qm_ref[hh, t0:t0 + rb, :])
                s = lax.dot_general(qa, ka_ref[0, 0:nk, :], NT, preferred_element_type=F32)
                if diag:
                    s = _causal_chunk(s, t0)
                _chunk_update(s, v_ref[0, 0:nk, :], m_ref, l_ref, acc_ref, 0, hh * tq + t0, rb)

    pl.when(j < i)(lambda: step(False))

    @pl.when(j == i)
    def _():
        step(True)
        _store_group(o_ref, acc_ref[0] / l_ref[0], low, tq)


def _sel_prompt(qn, selb, kaug, vdup, t, tq, rb, nl):
    n = t // tq
    qi, kj = _tri_steps(n)
    nsup = nl // LANES
    per_sup = LANES * SEL_BLOCK // tq
    gs = pltpu.PrefetchScalarGridSpec(
        num_scalar_prefetch=2, grid=(NSA_KV_HEADS, qi.shape[0]),
        in_specs=[pl.BlockSpec((tq, 256), lambda g, s, qi, kj: (qi[s], g)),
                  pl.BlockSpec((tq, LANES), lambda g, s, qi, kj: (qi[s], g * nsup + kj[s] // per_sup)),
                  pl.BlockSpec((1, tq, LANES), lambda g, s, qi, kj: (g, kj[s], 0)),
                  pl.BlockSpec((1, tq, LANES), lambda g, s, qi, kj: (g, kj[s], 0))],
        out_specs=pl.BlockSpec((tq, 256), lambda g, s, qi, kj: (qi[s], g)),
        scratch_shapes=[pltpu.VMEM((4, tq, LANES), BF), pltpu.VMEM((1, 4 * tq, LANES), F32),
                        pltpu.VMEM((1, 4 * tq, LANES), F32), pltpu.VMEM((1, 4 * tq, LANES), F32)])
    return pl.pallas_call(
        functools.partial(_selp_body, tq=tq, rb=rb), grid_spec=gs,
        out_shape=jax.ShapeDtypeStruct((t, 512), F32),
        compiler_params=_cparams("parallel", "arbitrary"), name="sel_prompt")(qi, kj, qn, selb, kaug, vdup)


def _winp_body(q_ref, *refs, tq, nprev):
    nt = nprev + 1
    k_refs, v_refs, o_ref = refs[:nt], refs[nt:2 * nt], refs[2 * nt]
    i = pl.program_id(1)
    low = lax.broadcasted_iota(jnp.int32, (tq, LANES), 1) < HEAD_DIM
    k = jnp.concatenate([r[0] for r in k_refs], axis=0)
    v = jnp.concatenate([r[0] for r in v_refs], axis=0)
    nk = k.shape[0]
    tpos = i * tq + lax.broadcasted_iota(jnp.int32, (tq, nk), 0)
    kpos = (i - nprev) * tq + lax.broadcasted_iota(jnp.int32, (tq, nk), 1)
    ok = (kpos <= tpos) & (tpos - kpos < WINDOW) & (kpos >= 0)
    outs = []
    for r in _group_rows(q_ref[...] * LOG2E, low):
        s = jnp.where(ok, lax.dot_general(r.astype(BF), k, NT, preferred_element_type=F32), NEG)
        e = jnp.exp2(s - jnp.max(s, axis=1, keepdims=True))
        outs.append(jnp.dot(e.astype(BF), v, preferred_element_type=F32) / jnp.sum(e, axis=1, keepdims=True))
    _store_group(o_ref, jnp.concatenate(outs, axis=0), low, tq)


def _win_prompt(qn, kwdup, vwdup, t, tq):
    nprev = WINDOW // tq
    kv = [pl.BlockSpec((1, tq, LANES), lambda g, i, dd=dd: (g, jnp.maximum(i - nprev + dd, 0), 0)) for dd in range(nprev + 1)]
    return pl.pallas_call(
        functools.partial(_winp_body, tq=tq, nprev=nprev), grid=(NSA_KV_HEADS, t // tq),
        in_specs=[pl.BlockSpec((tq, 256), lambda g, i: (i, g))] + kv + kv,
        out_specs=pl.BlockSpec((tq, 256), lambda g, i: (i, g)),
        out_shape=jax.ShapeDtypeStruct((t, 512), F32),
        compiler_params=_cparams("parallel", "parallel"), name="win_prompt")(
        qn, *([kwdup] * (nprev + 1)), *([vwdup] * (nprev + 1)))


ROWS_S = NSA_HEADS * NQS
ROWS_G = ROWS_S // NSA_KV_HEADS


def _new_key_mask(shape):
    r = lax.broadcasted_iota(jnp.int32, shape, 0) % NQS
    c = lax.broadcasted_iota(jnp.int32, shape, 1)
    return c <= r


def _slab_update(s, pv_of, m_ref, l_ref, acc_ref):
    m_prev = m_ref[0]
    m_new = jnp.maximum(m_prev, jnp.max(s, axis=1, keepdims=True))
    alpha = jnp.exp(m_prev - m_new)
    p = jnp.exp(s - m_new)
    l_ref[0] = alpha * l_ref[0] + jnp.sum(p, axis=1, keepdims=True)
    acc_ref[0] = alpha * acc_ref[0] + pv_of(p)
    m_ref[0] = m_new


def _slab_finish(sn, pv_of, m_ref, l_ref, acc_ref):
    m_prev = m_ref[0]
    m_new = jnp.maximum(m_prev, jnp.max(sn, axis=1, keepdims=True))
    alpha = jnp.exp(m_prev - m_new)
    pn = jnp.exp(sn - m_new)
    l = alpha * l_ref[0] + jnp.sum(pn, axis=1, keepdims=True)
    return (alpha * acc_ref[0] + pv_of(pn)) / l


def _slab_init(m_ref, l_ref, acc_ref):
    m_ref[...] = jnp.full_like(m_ref, NEG)
    l_ref[...] = jnp.zeros_like(l_ref)
    acc_ref[...] = jnp.zeros_like(acc_ref)


def _grp(a, g):
    return a[ROWS_G * g: ROWS_G * (g + 1)]


def _sels_body(pt_ref, q_ref, sb0_ref, sb1_ref, e_ref, kn_ref, vn_ref, *refs, pps, nsteps, per_chunk):
    del pt_ref
    pages = refs[:pps]
    o_ref, m_ref, l_ref, acc_ref = refs[pps:]
    sid = pl.program_id(1)
    pl.when(sid == 0)(lambda: _slab_init(m_ref, l_ref, acc_ref))

    q = q_ref[0] * SCALE
    qb = q.astype(BF)
    slab = lambda kind, g: jnp.concatenate([r[0, kind, g] for r in pages], axis=1).astype(BF)
    sbrows = jnp.concatenate([sb0_ref[0].astype(F32)] * 4 + [sb1_ref[0].astype(F32)] * 4, axis=0).astype(BF)
    bias = jnp.dot(sbrows, e_ref[sid % per_chunk], preferred_element_type=F32)
    s = jnp.concatenate([jnp.dot(_grp(qb, g), slab(0, g), preferred_element_type=F32)
                         for g in range(NSA_KV_HEADS)], axis=0) + bias
    _slab_update(s, lambda p: jnp.concatenate(
        [lax.dot_general(_grp(p, g).astype(BF), slab(1, g), NT, preferred_element_type=F32)
         for g in range(NSA_KV_HEADS)], axis=0), m_ref, l_ref, acc_ref)

    @pl.when(sid == nsteps - 1)
    def _():
        sn = jnp.concatenate([lax.dot_general(_grp(q, g), kn_ref[0, g], NT, preferred_element_type=F32)
                              for g in range(NSA_KV_HEADS)], axis=0)
        sn = jnp.where(_new_key_mask(sn.shape), sn, NEG)
        o_ref[0] = _slab_finish(sn, lambda pn: jnp.concatenate(
            [jnp.dot(_grp(pn, g), vn_ref[0, g], preferred_element_type=F32) for g in range(NSA_KV_HEADS)], axis=0),
            m_ref, l_ref, acc_ref)


def _sel_sample(qh, selb, kn, vn, pool_t, pt_flat, eexp, b, npg, pps, nl):
    nsteps = npg // pps
    per_chunk = eexp.shape[0]
    nchunk = nl // LANES
    pg = [pl.BlockSpec((1, 2, 2, HEAD_DIM, PAGE), lambda bb, s, pt, k=k: (pt[bb * npg + s * pps + k], 1, 0, 0, 0))
          for k in range(pps)]
    row = pl.BlockSpec((1, ROWS_S, HEAD_DIM), lambda bb, s, pt: (bb, 0, 0))
    new = pl.BlockSpec((1, NSA_KV_HEADS, NQS, HEAD_DIM), lambda bb, s, pt: (bb, 0, 0, 0))
    gs = pltpu.PrefetchScalarGridSpec(
        num_scalar_prefetch=1, grid=(b, nsteps),
        in_specs=[row,
                  pl.BlockSpec((1, NQS, LANES), lambda bb, s, pt: (bb, 0, s // per_chunk)),
                  pl.BlockSpec((1, NQS, LANES), lambda bb, s, pt: (bb, 0, nchunk + s // per_chunk)),
                  pl.BlockSpec(eexp.shape, lambda bb, s, pt: (0, 0, 0), pipeline_mode=pl.Buffered(1)),
                  new, new] + pg,
        out_specs=row,
        scratch_shapes=[pltpu.VMEM((1, ROWS_S, 1), F32), pltpu.VMEM((1, ROWS_S, 1), F32),
                        pltpu.VMEM((1, ROWS_S, HEAD_DIM), F32)])
    return pl.pallas_call(
        functools.partial(_sels_body, pps=pps, nsteps=nsteps, per_chunk=per_chunk), grid_spec=gs,
        out_shape=jax.ShapeDtypeStruct((b, ROWS_S, HEAD_DIM), F32),
        compiler_params=_cparams("parallel", "arbitrary"), name="sel_sample")(
        pt_flat, qh, selb, selb, eexp, kn, vn, *([pool_t] * pps))


def _wins_body(q_ref, st_ref, kn_ref, vn_ref, o_ref, *, wb):
    q = q_ref[0] * SCALE
    qb = q.astype(BF)
    s = jnp.concatenate([jnp.dot(_grp(qb, g), st_ref[0, 0, g].astype(BF), preferred_element_type=F32)
                         for g in range(NSA_KV_HEADS)], axis=0)
    tq = lax.broadcasted_iota(jnp.int32, s.shape, 0) % NQS
    n = lax.broadcasted_iota(jnp.int32, s.shape, 1)
    s = jnp.where(n > tq + (wb - WINDOW), s, NEG)
    sn = jnp.concatenate([lax.dot_general(_grp(q, g), kn_ref[0, g], NT, preferred_element_type=F32)
                          for g in range(NSA_KV_HEADS)], axis=0)
    sn = jnp.where(_new_key_mask(sn.shape), sn, NEG)
    m = jnp.maximum(jnp.max(s, axis=1, keepdims=True), jnp.max(sn, axis=1, keepdims=True))
    e = jnp.exp(s - m)
    en = jnp.exp(sn - m)
    den = jnp.sum(e, axis=1, keepdims=True) + jnp.sum(en, axis=1, keepdims=True)
    o = jnp.concatenate(
        [lax.dot_general(_grp(e, g).astype(BF), st_ref[0, 1, g].astype(BF), NT, preferred_element_type=F32)
         + jnp.dot(_grp(en, g), vn_ref[0, g], preferred_element_type=F32) for g in range(NSA_KV_HEADS)], axis=0)
    o_ref[0] = o / den


def _win_sample(qh, state_t, kn, vn, b, wb):
    row = pl.BlockSpec((1, ROWS_S, HEAD_DIM), lambda bb: (bb, 0, 0))
    new = pl.BlockSpec((1, NSA_KV_HEADS, NQS, HEAD_DIM), lambda bb: (bb, 0, 0, 0))
    return pl.pallas_call(
        functools.partial(_wins_body, wb=wb), grid=(b,),
        in_specs=[row, pl.BlockSpec((1, 2, NSA_KV_HEADS, HEAD_DIM, wb), lambda bb: (bb, 0, 0, 0, 0)), new, new],
        out_specs=row, out_shape=jax.ShapeDtypeStruct((b, ROWS_S, HEAD_DIM), F32),
        compiler_params=_cparams("parallel"), name="win_sample")(qh, state_t, kn, vn)


def _foxs_body(pt_ref, q_ref, kn_ref, vn_ref, lfn_ref, u_ref, *refs, pps, nsteps):
    del pt_ref
    kv = refs[:pps]
    lfp = refs[pps:2 * pps]
    o_ref, m_ref, l_ref, acc_ref, carry_ref = refs[2 * pps:]
    sid = pl.program_id(1)

    @pl.when(sid == 0)
    def _():
        _slab_init(m_ref, l_ref, acc_ref)
        carry_ref[...] = jnp.zeros_like(carry_ref)

    q = q_ref[0] * SCALE
    qb = q.astype(BF)
    head = lambda a, h: a[NQS * h: NQS * (h + 1)]
    slab = lambda kind, h: jnp.concatenate([r[0, kind, h] for r in kv], axis=1).astype(BF)
    lft = jnp.concatenate([r[0] for r in lfp], axis=1)
    cum = _dot3(lft, u_ref[...]) + carry_ref[:, 0:1]
    keys = cum.shape[1]
    carry_ref[...] = jnp.broadcast_to(cum[:, keys - 1:], carry_ref.shape)
    s = jnp.concatenate([jnp.dot(head(qb, h), slab(0, h), preferred_element_type=F32)
                         - jnp.broadcast_to(cum[h:h + 1, :], (NQS, keys)) for h in range(FOX_HEADS)], axis=0)
    _slab_update(s, lambda p: jnp.concatenate(
        [lax.dot_general(head(p, h).astype(BF), slab(1, h), NT, preferred_element_type=F32)
         for h in range(FOX_HEADS)], axis=0), m_ref, l_ref, acc_ref)

    @pl.when(sid == nsteps - 1)
    def _():
        lfn = lfn_ref[0]
        col = lax.broadcasted_iota(jnp.int32, (FOX_HEADS, NQS), 1)
        cn = carry_ref[:, 0:NQS]
        for i in range(NQS):
            cn = cn + jnp.where(col >= i, lfn[:, i:i + 1], 0.0)
        sn = jnp.concatenate([lax.dot_general(head(q, h), kn_ref[0, h], NT, preferred_element_type=F32)
                              - jnp.broadcast_to(cn[h:h + 1, :], (NQS, NQS)) for h in range(FOX_HEADS)], axis=0)
        sn = jnp.where(_new_key_mask(sn.shape), sn, NEG)
        o_ref[0] = _slab_finish(sn, lambda pn: jnp.concatenate(
            [jnp.dot(head(pn, h), vn_ref[0, h], preferred_element_type=F32) for h in range(FOX_HEADS)], axis=0),
            m_ref, l_ref, acc_ref)


def _fox_sample(qh, kn, vn, lfn, pool_t, lf_t, pt_flat, b, npg, pps):
    nsteps = npg // pps
    keys = pps * PAGE
    u = jnp.asarray(np.triu(np.ones((keys, keys), np.float32)), BF)
    kvs = [pl.BlockSpec((1, 2, FOX_HEADS, HEAD_DIM, PAGE), lambda bb, s, pt, k=k: (pt[bb * npg + s * pps + k], 0, 0, 0, 0))
           for k in range(pps)]
    lfs = [pl.BlockSpec((1, FOX_HEADS, PAGE), lambda bb, s, pt, k=k: (pt[bb * npg + s * pps + k], 0, 0)) for k in range(pps)]
    row = pl.BlockSpec((1, ROWS_S, HEAD_DIM), lambda bb, s, pt: (bb, 0, 0))
    new = pl.BlockSpec((1, FOX_HEADS, NQS, HEAD_DIM), lambda bb, s, pt: (bb, 0, 0, 0))
    gs = pltpu.PrefetchScalarGridSpec(
        num_scalar_prefetch=1, grid=(b, nsteps),
        in_specs=[row, new, new, pl.BlockSpec((1, FOX_HEADS, NQS), lambda bb, s, pt: (bb, 0, 0)),
                  pl.BlockSpec((keys, keys), lambda bb, s, pt: (0, 0), pipeline_mode=pl.Buffered(1))] + kvs + lfs,
        out_specs=row,
        scratch_shapes=[pltpu.VMEM((1, ROWS_S, 1), F32), pltpu.VMEM((1, ROWS_S, 1), F32),
                        pltpu.VMEM((1, ROWS_S, HEAD_DIM), F32), pltpu.VMEM((FOX_HEADS, LANES), F32)])
    return pl.pallas_call(
        functools.partial(_foxs_body, pps=pps, nsteps=nsteps), grid_spec=gs,
        out_shape=jax.ShapeDtypeStruct((b, ROWS_S, HEAD_DIM), F32),
        compiler_params=_cparams("parallel", "arbitrary"), name="fox_sample")(
        pt_flat, qh, kn, vn, lfn, u, *([pool_t] * pps), *([lf_t] * pps))


def _rope_tables(pos):
    half = ROT_DIM // 2
    inv = ROPE_THETA ** (-(jnp.arange(half, dtype=F32) * 2.0 / ROT_DIM))
    ang = pos.astype(F32)[:, None] * inv[None, :]
    cos, sin = jnp.cos(ang), jnp.sin(ang)
    n = pos.shape[0]
    c64 = jnp.concatenate([cos, cos, jnp.ones((n, HEAD_DIM - ROT_DIM), F32)], axis=1)
    s64 = jnp.concatenate([-sin, sin, jnp.zeros((n, HEAD_DIM - ROT_DIM), F32)], axis=1)
    return jnp.tile(c64, (1, 2)), jnp.tile(s64, (1, 2))


def _importance_matrix(nc, nl):
    m = np.zeros((nl, nc), np.float32)
    per = SEL_BLOCK // CMP_STRIDE
    for n in range(nc - 1):
        m[n // per, n] += 1.0
        m[(n + 1) // per, n] += 1.0
    return jnp.asarray(m, BF)


def _gate_expand():
    e = np.zeros((3, LANES, 512), np.float32)
    for i in range(3):
        for h in range(NSA_HEADS):
            e[i, 3 * h + i, HEAD_DIM * h: HEAD_DIM * (h + 1)] = 1.0
    return jnp.asarray(e, BF)


def _block_expand(pps):
    per_step = pps * PAGE // SEL_BLOCK
    per_chunk = LANES // per_step
    e = np.zeros((per_chunk, LANES, pps * PAGE), np.float32)
    for r in range(per_chunk):
        for key in range(pps * PAGE):
            e[r, r * per_step + key // SEL_BLOCK, key] = 1.0
    return jnp.asarray(e, BF)


def _cmp_weights(w1):
    top, bot = w1[:CMP_STRIDE], w1[CMP_STRIDE:]
    wl = jnp.concatenate([top, bot], axis=2)
    z = jnp.zeros_like(wl)
    w = jnp.concatenate([jnp.concatenate([wl, z], axis=2), jnp.concatenate([z, wl], axis=2)], axis=1)
    return w.reshape(CMP_STRIDE * LANES, 512).astype(BF)


def _pad_rows(a, rp):
    return jnp.pad(a, ((0, rp - a.shape[0]),) + ((0, 0),) * (a.ndim - 1))


def kernel(x_prompt, x_sample, cache_nsa_kv, cache_fox_kv, cache_fox_logf, state_nsa_win_kv, page_table, norm_ffn1, w1_gate, w1_up, w1_down, norm_mix, w_in, b_forget, nsa_q_norm, nsa_k_norm, cmp_pe_k, cmp_w1_k, cmp_w2_k, cmp_pe_v, cmp_w1_v, cmp_w2_v, fox_q_norm, fox_k_norm, w_branch_nsa, w_branch_fox, w_out, norm_ffn2, w2_gate, w2_up, w2_down):
    bp, t, d = x_prompt.shape
    b, ts, _ = x_sample.shape
    assert bp == 1 and norm_ffn1.shape[0] == 1 and ts <= 4
    npg = page_table.shape[1]
    past = npg * PAGE
    n_pool = cache_nsa_kv.shape[1]
    wb = state_nsa_win_kv.shape[2]
    tm = 256
    tq_fox = min(1024, t)
    tq_sel = min(1024, t)
    rb = 1024
    tq_win = 512
    nq_cmp = 128
    pps_cmp = math.gcd(math.gcd(t // PAGE, npg), 32)
    pps_sel, pps_fox = math.gcd(npg, 16), math.gcd(npg, 8)
    assert t % PAGE == 0 and t % tq_fox == 0 and t % tq_sel == 0 and t % tq_win == 0 and WINDOW % tq_win == 0
    assert (HEAD_DIM * SEL_BLOCK) % tq_sel == 0 and LANES % (2 * pps_sel) == 0

    rs = b * NQS
    tm_s = tm if rs >= tm else rs
    rsp = -(-rs // tm_s) * tm_s
    assert t % tm == 0
    xs = _pad_rows(jnp.pad(x_sample, ((0, 0), (0, NQS - ts), (0, 0))).reshape(rs, d), rsp)
    pos_s = jnp.concatenate([past + (jnp.arange(rs) % NQS), jnp.zeros((rsp - rs,), jnp.int32)])

    f = w1_gate.shape[2]
    fp = -(-f // LANES) * LANES
    padc = lambda w: jnp.pad(w.astype(BF), ((0, 0), (0, fp - f)))
    padr = lambda w: jnp.pad(w.astype(BF), ((0, fp - f), (0, 0)))
    wi = w_in[0]
    o2, o3, o4, o5 = 1280, 1304, 2840, 2848
    wa = jnp.concatenate([wi[:, :o2], wi[:, o3:o4], wi[:, o5:], wi[:, o2:o3], wi[:, o4:o5],
                          jnp.zeros((d, LANES - 32), F32)], axis=1).astype(BF)
    bm = jnp.zeros((1, LANES), F32).at[0, 24:32].set(b_forget[0])
    g2 = lambda g: jnp.tile(g, 2)[None, :]
    seg = jnp.asarray(np.kron(np.eye(2, dtype=np.float32), np.full((HEAD_DIM, HEAD_DIM), 1.0 / HEAD_DIM, np.float32)), BF)

    ffn1_w = (norm_ffn1, padc(w1_gate[0]), padc(w1_up[0]), padr(w1_down[0]))
    ffn2_w = (norm_ffn2, padc(w2_gate[0]), padc(w2_up[0]), padr(w2_down[0]))
    proj_w = (norm_mix, wa, seg, g2(nsa_q_norm[0]), g2(nsa_k_norm[0, 1]), g2(nsa_k_norm[0, 2]), g2(fox_q_norm[0]),
              g2(fox_k_norm[0]), bm)

    def front(x, pos, tmx):
        x1 = _ffn(x, *ffn1_w, tmx)
        return (x1,) + tuple(_proj(x1, *proj_w, *_rope_tables(pos), tmx))

    (x1, qn, nsa_rows, win_rows, qf, fox_rows, gm, misc, foxb, kaug, vdup, kwdup, vwdup) = front(
        x_prompt[0], jnp.arange(t), tm)
    (x1_s, qn_n, nsa_new, win_new_r, qf_n, fox_new, gm_s, misc_s) = front(xs, pos_s, tm_s)[:8]

    ct = _cumsum_t(misc, t, min(512, t)).reshape(FOX_HEADS // 2, 2, t)
    o_fox_p = _fox_prompt(qf, foxb, ct, t, tq_fox, rb)

    wk, wv = _cmp_weights(cmp_w1_k[0]), _cmp_weights(cmp_w1_v[0])
    kdim = CMP_BLOCK * HEAD_DIM
    pe8 = lambda pe: jnp.broadcast_to(pe.reshape(1, kdim), (8, kdim))
    w2d = lambda w2: jnp.concatenate([w2, w2], axis=1).astype(BF)
    gkc = g2(nsa_k_norm[0, 0])

    def compress(ab_fn, pool, pt_flat, nseq, nc):
        abk, abv = ab_fn(pool, pt_flat, wk, wv, pps_cmp)
        posc = jnp.arange(nc) * CMP_STRIDE + (CMP_BLOCK - 1)
        cosc, sinc = _rope_tables(posc)
        return _cmp_fin(abk, abv, pe8(cmp_pe_k[0]), cmp_w1_k[0].reshape(kdim, CMP_HIDDEN), w2d(cmp_w2_k[0]),
                        pe8(cmp_pe_v[0]), cmp_w1_v[0].reshape(kdim, CMP_HIDDEN), w2d(cmp_w2_v[0]), seg, gkc, cosc, sinc,
                        nseq, nc)

    nsub = PAGE // CMP_STRIDE
    tpg = t // PAGE
    kc_p, vc_p = compress(_cmp_ab, nsa_rows.reshape(tpg, PAGE, 512), jnp.arange(tpg, dtype=jnp.int32), 1, tpg * nsub)
    nsa_t = jnp.transpose(cache_nsa_kv[0], (0, 2, 3, 4, 1))
    fox_t = jnp.transpose(cache_fox_kv[0], (0, 2, 3, 4, 1))
    lf_t = jnp.transpose(cache_fox_logf[0], (0, 2, 1))
    st_t = jnp.transpose(state_nsa_win_kv[0], (0, 2, 3, 4, 1))
    pt_flat = page_table.reshape(-1).astype(jnp.int32)
    kc_s, vc_s = compress(_cmp_ab_t, nsa_t, pt_flat, b, npg * nsub)

    nl_p = -(-(t // SEL_BLOCK) // LANES) * LANES
    nl_s = -(-(past // SEL_BLOCK) // LANES) * LANES
    nbp = t // nq_cmp
    oc_p, sb_p = _cmp_attn(qn.reshape(nbp, nq_cmp, 512), kc_p, vc_p, _importance_matrix(tpg * nsub, nl_p),
                           jnp.zeros((nbp,), jnp.int32), jnp.arange(nbp, dtype=jnp.int32) * nq_cmp, nq_cmp,
                           t // SEL_BLOCK, min(N_SEL, t // SEL_BLOCK))
    new = lambda a: a[:rs]
    qn_s = new(qn_n).reshape(b, NQS, 512)
    oc_s, sb_s = _cmp_attn(qn_s, kc_s, vc_s, _importance_matrix(npg * nsub, nl_s), jnp.arange(b, dtype=jnp.int32),
                           jnp.full((b,), past, jnp.int32), NQS, past // SEL_BLOCK,
                           min(N_SEL, past // SEL_BLOCK + 1) - 1)

    os_p = _sel_prompt(qn, sb_p.reshape(t, 2 * nl_p), kaug, vdup, t, tq_sel, rb, nl_p)
    ow_p = _win_prompt(qn, kwdup, vwdup, t, tq_win)

    hm = lambda a, nh: new(a).reshape(b, NQS, nh, HEAD_DIM).transpose(0, 2, 1, 3)
    rows_of = lambda o: _pad_rows(o.reshape(b, NSA_HEADS, NQS, HEAD_DIM).transpose(0, 2, 1, 3).reshape(rs, 512), rsp)
    qn_h = hm(qn_n, NSA_HEADS).reshape(b, ROWS_S, HEAD_DIM)
    os_s = _sel_sample(qn_h, sb_s, hm(nsa_new[:, 256:384], NSA_KV_HEADS), hm(nsa_new[:, 384:512], NSA_KV_HEADS), nsa_t,
                       pt_flat, _block_expand(pps_sel), b, npg, pps_sel, nl_s)
    ow_s = _win_sample(qn_h, st_t, hm(win_new_r[:, 0:128], NSA_KV_HEADS), hm(win_new_r[:, 128:256], NSA_KV_HEADS), b, wb)
    of_s = _fox_sample(hm(qf_n, FOX_HEADS).reshape(b, ROWS_S, HEAD_DIM), hm(fox_new[:, 0:512], FOX_HEADS),
                       hm(fox_new[:, 512:1024], FOX_HEADS), new(misc_s)[:, 24:32].reshape(b, NQS, FOX_HEADS).transpose(0, 2, 1),
                       fox_t, lf_t, pt_flat, b, npg, pps_fox)

    merge_w = (_gate_expand(), w_branch_nsa[0].astype(BF), w_branch_fox[0].astype(BF), w_out[0].astype(BF))
    y = _ffn(_merge(x1, oc_p.reshape(t, 512), os_p, ow_p, o_fox_p, misc, gm, *merge_w, tm), *ffn2_w, tm)
    y_s = _ffn(_merge(x1_s, _pad_rows(oc_s.reshape(rs, 512), rsp), rows_of(os_s), rows_of(ow_s), rows_of(of_s), misc_s,
                      gm_s, *merge_w, tm_s), *ffn2_w, tm_s)

    smp = lambda a, *shape: new(a).reshape((b, NQS) + shape)[:, :ts]
    g, hd = NSA_KV_HEADS, HEAD_DIM
    win_new = smp(win_new_r, 2, g, hd)
    win_all_s = jnp.concatenate([state_nsa_win_kv[0], win_new], axis=1)
    keep_p = min(WINDOW, t)
    keep_s = min(WINDOW, wb + ts)
    return (y[None], smp(y_s, d),
            nsa_rows.reshape(1, 1, t, 4, g, hd), smp(nsa_new, 4, g, hd)[None],
            fox_rows.reshape(1, 1, t, 2, FOX_HEADS, hd), smp(fox_new, 2, FOX_HEADS, hd)[None],
            misc[:, 24:32].reshape(1, 1, t, FOX_HEADS), smp(misc_s[:, 24:32], FOX_HEADS)[None],
            win_rows[t - keep_p:].reshape(1, 1, keep_p, 2, g, hd), win_all_s[:, wb + ts - keep_s:][None])
```

```python
import functools
import math

import numpy as np
import jax
import jax.numpy as jnp
from jax import lax
from jax.experimental import pallas as pl
from jax.experimental.pallas import tpu as pltpu

F32 = jnp.float32
BF = jnp.bfloat16

HEAD_DIM = 64
NSA_HEADS = 8
NSA_KV_HEADS = 2
FOX_HEADS = 8
ROT_DIM = 16
ROPE_THETA = 500000.0
CMP_BLOCK = 32
CMP_STRIDE = 16
CMP_HIDDEN = 128
SEL_BLOCK = 64
N_SEL = 16
WINDOW = 512
PAGE = 128
EPS = 1e-6
FORCE_SCORE = 1e6
SCALE = 0.125
LOG2E = 1.4426950408889634

LANES = 128
NQS = 8
NEG = -1e30
SEL_OFF = -float(2 ** 30)
VMEM_LIMIT = 56 * 1024 * 1024

NT = (((1,), (1,)), ((), ()))


def _cparams(*sem):
    return pltpu.CompilerParams(dimension_semantics=sem, vmem_limit_bytes=VMEM_LIMIT)


def _const_spec(shape):
    return pl.BlockSpec(shape, lambda *_: (0,) * len(shape), pipeline_mode=pl.Buffered(1))


def _split3(a):
    hi = a.astype(BF)
    r1 = a - hi.astype(F32)
    mid = r1.astype(BF)
    lo = (r1 - mid.astype(F32)).astype(BF)
    return hi, mid, lo


def _dot3(a, m):
    hi, mid, lo = _split3(a)
    return (jnp.dot(hi, m, preferred_element_type=F32) + jnp.dot(mid, m, preferred_element_type=F32)
            + jnp.dot(lo, m, preferred_element_type=F32))


def _rms_rows(x, gain):
    return x * lax.rsqrt(jnp.mean(x * x, axis=-1, keepdims=True) + EPS) * gain


def _ffn_body(x_ref, n_ref, wg_ref, wu_ref, wd_ref, o_ref):
    x = x_ref[...]
    h = _rms_rows(x, n_ref[...]).astype(BF)
    g = jnp.dot(h, wg_ref[...], preferred_element_type=F32)
    u = jnp.dot(h, wu_ref[...], preferred_element_type=F32)
    a = (g * jax.nn.sigmoid(g) * u).astype(BF)
    o_ref[...] = x + 0.5 * jnp.dot(a, wd_ref[...], preferred_element_type=F32)


def _ffn(x, n, wg, wu, wd, tm):
    rp, d = x.shape
    f = wg.shape[1]
    row = pl.BlockSpec((tm, d), lambda i: (i, 0))
    return pl.pallas_call(
        _ffn_body, grid=(rp // tm,),
        in_specs=[row, _const_spec((1, d)), _const_spec((d, f)), _const_spec((d, f)), _const_spec((f, d))],
        out_specs=row, out_shape=jax.ShapeDtypeStruct((rp, d), F32),
        compiler_params=_cparams("parallel"), name="ffn")(x, n, wg, wu, wd)


def _proj_body(x_ref, nm_ref, w_ref, seg_ref, gq_ref, gks_ref, gkw_ref, gfq_ref, gfk_ref, bm_ref, c_ref, s_ref,
               qn_ref, nsa_ref, win_ref, qf_ref, fox_ref, gm_ref, misc_ref, foxb_ref, kaug_ref, vdup_ref, kwdup_ref,
               vwdup_ref, *, tm, d):
    x = x_ref[...]
    h = _rms_rows(x, nm_ref[...]).astype(BF)
    z = jnp.dot(h, w_ref[...], preferred_element_type=F32)
    cosv = c_ref[...]
    sinv = s_ref[...]
    seg = seg_ref[...]
    lane = lax.broadcasted_iota(jnp.int32, (tm, LANES), 1)
    first8 = (lane % HEAD_DIM) < (ROT_DIM // 2)
    low = lane < HEAD_DIM

    def hnorm(v, gain):
        zz = v * v
        hi = zz.astype(BF)
        lo = (zz - hi.astype(F32)).astype(BF)
        ms = jnp.dot(hi, seg, preferred_element_type=F32) + jnp.dot(lo, seg, preferred_element_type=F32)
        return v * lax.rsqrt(ms + EPS) * gain

    def rope(v):
        partner = jnp.where(first8, pltpu.roll(v, LANES - ROT_DIM // 2, 1), pltpu.roll(v, ROT_DIM // 2, 1))
        return v * cosv + partner * sinv

    def dup(v):
        r = pltpu.roll(v, HEAD_DIM, 1)
        return jnp.where(low, v, r), jnp.where(low, r, v)

    def chunk(off, c):
        return z[:, off + LANES * c: off + LANES * (c + 1)]

    o_q, o_kv, o_f, o_gm = 0, 512, 1280, 2816
    o_misc = o_gm + 2 * d
    for c in range(4):
        qn_ref[:, LANES * c: LANES * (c + 1)] = rope(hnorm(chunk(o_q, c), gq_ref[...]))
        qf_ref[:, LANES * c: LANES * (c + 1)] = hnorm(chunk(o_f, c), gfq_ref[...])
        fk = hnorm(chunk(o_f + 512, c), gfk_ref[...])
        fox_ref[:, LANES * c: LANES * (c + 1)] = fk
        foxb_ref[:, LANES * c: LANES * (c + 1)] = fk.astype(BF)
    fv = z[:, o_f + 1024: o_f + 1536]
    fox_ref[:, 512:1024] = fv
    foxb_ref[:, 512:1024] = fv.astype(BF)
    nsa_ref[:, 0:256] = z[:, o_kv: o_kv + 256]
    ks = rope(hnorm(chunk(o_kv, 2), gks_ref[...]))
    vs = chunk(o_kv, 3)
    kw = rope(hnorm(chunk(o_kv, 4), gkw_ref[...]))
    vw = chunk(o_kv, 5)
    nsa_ref[:, 256:384] = ks
    nsa_ref[:, 384:512] = vs
    win_ref[:, 0:128] = kw
    win_ref[:, 128:256] = vw
    gm_ref[...] = jax.nn.sigmoid(z[:, o_gm: o_gm + 2 * d])
    zm = z[:, o_misc: o_misc + LANES] + bm_ref[...]
    logsig = jnp.minimum(zm, 0.0) - jnp.log1p(jnp.exp(-jnp.abs(zm)))
    misc_ref[...] = jnp.where(lane < 3 * NSA_HEADS, jax.nn.sigmoid(zm), logsig)
    pos = pl.program_id(0) * tm + lax.broadcasted_iota(jnp.int32, (tm, 1), 0)
    blk = pos // SEL_BLOCK
    onehot = jnp.where(blk % LANES == lane, 1.0, 0.0)
    hot_half = (lane // HEAD_DIM) == (blk // HEAD_DIM) % 2
    k0, k1 = dup(ks * SCALE)
    v0, v1 = dup(vs)
    kw0, kw1 = dup(kw * SCALE)
    vw0, vw1 = dup(vw)
    for g, (kk, vv, kkw, vvw) in enumerate(((k0, v0, kw0, vw0), (k1, v1, kw1, vw1))):
        kaug_ref[g] = jnp.where(hot_half, onehot, kk).astype(BF)
        vdup_ref[g] = vv.astype(BF)
        kwdup_ref[g] = kkw.astype(BF)
        vwdup_ref[g] = vvw.astype(BF)


def _proj(x, nm, wa, seg, gq, gks, gkw, gfq, gfk, bm, cosv, sinv, tm):
    rp, d = x.shape
    n = wa.shape[1]
    row = lambda w: pl.BlockSpec((tm, w), lambda i: (i, 0))
    grp = lambda w: pl.BlockSpec((2, tm, w), lambda i: (0, i, 0))
    g128 = _const_spec((1, LANES))
    outs = [(512, F32), (512, F32), (256, F32), (512, F32), (1024, F32), (2 * d, F32), (LANES, F32), (1024, BF)]
    out_shape = [jax.ShapeDtypeStruct((rp, w), dt) for w, dt in outs]
    out_shape += [jax.ShapeDtypeStruct((2, rp, LANES), BF) for _ in range(4)]
    out_specs = [row(w) for w, _ in outs] + [grp(LANES) for _ in range(4)]
    return pl.pallas_call(
        functools.partial(_proj_body, tm=tm, d=d), grid=(rp // tm,),
        in_specs=[row(d), _const_spec((1, d)), _const_spec((d, n)), _const_spec((LANES, LANES)),
                  g128, g128, g128, g128, g128, g128, row(LANES), row(LANES)],
        out_specs=out_specs, out_shape=out_shape,
        compiler_params=_cparams("parallel"), name="proj")(x, nm, wa, seg, gq, gks, gkw, gfq, gfk, bm, cosv, sinv)


def _merge_body(x_ref, oc_ref, os_ref, ow_ref, of_ref, misc_ref, gm_ref, e_ref, wbn_ref, wbf_ref, wo_ref, o_ref, *, d):
    hi, mid, lo = _split3(misc_ref[...])

    def gate(i):
        e = e_ref[i]
        return (jnp.dot(hi, e, preferred_element_type=F32) + jnp.dot(mid, e, preferred_element_type=F32)
                + jnp.dot(lo, e, preferred_element_type=F32))

    o_nsa = gate(0) * oc_ref[...] + gate(1) * os_ref[...] + gate(2) * ow_ref[...]
    a = jnp.dot(o_nsa.astype(BF), wbn_ref[...], preferred_element_type=F32)
    b = jnp.dot(of_ref[...].astype(BF), wbf_ref[...], preferred_element_type=F32)
    gm = gm_ref[...]
    mixed = gm[:, :d] * a + gm[:, d:] * b
    o_ref[...] = x_ref[...] + jnp.dot(mixed.astype(BF), wo_ref[...], preferred_element_type=F32)


def _merge(x, oc, os_, ow, of, misc, gm, e, wbn, wbf, wo, tm):
    rp, d = x.shape
    row = lambda w: pl.BlockSpec((tm, w), lambda i: (i, 0))
    return pl.pallas_call(
        functools.partial(_merge_body, d=d), grid=(rp // tm,),
        in_specs=[row(d), row(512), row(512), row(512), row(512), row(LANES), row(2 * d),
                  _const_spec((3, LANES, 512)), _const_spec((512, d)), _const_spec((512, d)), _const_spec((d, d))],
        out_specs=row(d), out_shape=jax.ShapeDtypeStruct((rp, d), F32),
        compiler_params=_cparams("parallel"), name="merge")(x, oc, os_, ow, of, misc, gm, e, wbn, wbf, wo)


def _cumsum_body(m_ref, u_ref, o_ref, carry_ref, *, tb):
    @pl.when(pl.program_id(0) == 0)
    def _():
        carry_ref[...] = jnp.zeros_like(carry_ref)

    lft = m_ref[...].T
    a = lft[3 * NSA_HEADS: 3 * NSA_HEADS + FOX_HEADS, :]
    c = _dot3(a, u_ref[...]) + carry_ref[:, 0:1]
    o_ref[...] = c * LOG2E
    carry_ref[...] = jnp.broadcast_to(c[:, tb - 1: tb], carry_ref.shape)


def _cumsum_t(misc, t, tb):
    u = jnp.asarray(np.triu(np.ones((tb, tb), np.float32)), BF)
    return pl.pallas_call(
        functools.partial(_cumsum_body, tb=tb), grid=(t // tb,),
        in_specs=[pl.BlockSpec((tb, LANES), lambda i: (i, 0)), _const_spec((tb, tb))],
        out_specs=pl.BlockSpec((FOX_HEADS, tb), lambda i: (0, i)),
        out_shape=jax.ShapeDtypeStruct((FOX_HEADS, t), F32),
        scratch_shapes=[pltpu.VMEM((FOX_HEADS, LANES), F32)],
        compiler_params=_cparams("arbitrary"), name="fox_cumsum")(misc, u)


def _tri_steps(n):
    qi = np.concatenate([np.full(i + 1, i, np.int32) for i in range(n)])
    kj = np.concatenate([np.arange(i + 1, dtype=np.int32) for i in range(n)])
    return jnp.asarray(qi), jnp.asarray(kj)


def _chunk_update(s, v, m_ref, l_ref, acc_ref, idx, r0, rb):
    rows = slice(r0, r0 + rb)
    m_prev = m_ref[idx, rows, :]
    m_new = jnp.maximum(m_prev, jnp.max(s, axis=1, keepdims=True))
    alpha = jnp.exp2(m_prev - m_new)
    p = jnp.exp2(s - jnp.tile(m_new, (1, s.shape[1] // LANES)))
    l_ref[idx, rows, :] = alpha * l_ref[idx, rows, :] + jnp.sum(p, axis=1, keepdims=True)
    acc_ref[idx, rows, :] = alpha * acc_ref[idx, rows, :] + jnp.dot(p.astype(BF), v, preferred_element_type=F32)
    m_ref[idx, rows, :] = m_new


def _causal_chunk(s, r0):
    r = r0 + lax.broadcasted_iota(jnp.int32, s.shape, 0)
    c = lax.broadcasted_iota(jnp.int32, s.shape, 1)
    return jnp.where(c <= r, s, NEG)


def _foxp_body(qi_ref, kj_ref, q_ref, k_ref, v_ref, c_ref, o_ref, qm_ref, m_ref, l_ref, acc_ref, *, tb, rb):
    sid = pl.program_id(1)
    i = qi_ref[sid]
    j = kj_ref[sid]
    low = lax.broadcasted_iota(jnp.int32, (tb, LANES), 1) < HEAD_DIM

    @pl.when(j == 0)
    def _():
        q = q_ref[...] * (SCALE * LOG2E)
        qm_ref[0] = jnp.where(low, q, 0.0).astype(BF)
        qm_ref[1] = jnp.where(low, 0.0, q).astype(BF)
        m_ref[...] = jnp.full_like(m_ref, NEG)
        l_ref[...] = jnp.zeros_like(l_ref)
        acc_ref[...] = jnp.zeros_like(acc_ref)

    def step(diag):
        for hh in range(2):
            for r0 in range(0, tb, rb):
                nk = r0 + rb if diag else tb
                s = (lax.dot_general(qm_ref[hh, r0:r0 + rb, :], k_ref[0:nk, :], NT, preferred_element_type=F32)
                     - c_ref[0, hh:hh + 1, 0:nk])
                if diag:
                    s = _causal_chunk(s, r0)
                _chunk_update(s, v_ref[0:nk, :], m_ref, l_ref, acc_ref, hh, r0, rb)

    pl.when(j < i)(lambda: step(False))

    @pl.when(j == i)
    def _():
        step(True)
        o_ref[...] = jnp.where(low, acc_ref[0] / l_ref[0], acc_ref[1] / l_ref[1])


def _fox_prompt(qf, foxb, ct, t, tb, rb):
    n = t // tb
    qi, kj = _tri_steps(n)
    npair = FOX_HEADS // 2
    gs = pltpu.PrefetchScalarGridSpec(
        num_scalar_prefetch=2, grid=(npair, qi.shape[0]),
        in_specs=[pl.BlockSpec((tb, LANES), lambda p, s, qi, kj: (qi[s], p)),
                  pl.BlockSpec((tb, LANES), lambda p, s, qi, kj: (kj[s], p)),
                  pl.BlockSpec((tb, LANES), lambda p, s, qi, kj: (kj[s], npair + p)),
                  pl.BlockSpec((1, 2, tb), lambda p, s, qi, kj: (p, 0, kj[s]))],
        out_specs=pl.BlockSpec((tb, LANES), lambda p, s, qi, kj: (qi[s], p)),
        scratch_shapes=[pltpu.VMEM((2, tb, LANES), BF), pltpu.VMEM((2, tb, LANES), F32),
                        pltpu.VMEM((2, tb, LANES), F32), pltpu.VMEM((2, tb, LANES), F32)])
    return pl.pallas_call(
        functools.partial(_foxp_body, tb=tb, rb=rb), grid_spec=gs,
        out_shape=jax.ShapeDtypeStruct((t, 512), F32),
        compiler_params=_cparams("parallel", "arbitrary"), name="fox_prompt")(qi, kj, qf, foxb, foxb, ct)


def _cmp_accumulate(rows_k, rows_v, wk_ref, wv_ref, abk_ref, abv_ref):
    xk = jnp.concatenate([rows_k(l) for l in range(CMP_STRIDE)], axis=1).astype(BF)
    xv = jnp.concatenate([rows_v(l) for l in range(CMP_STRIDE)], axis=1).astype(BF)
    abk_ref[...] = jnp.dot(xk, wk_ref[...], preferred_element_type=F32)
    abv_ref[...] = jnp.dot(xv, wv_ref[...], preferred_element_type=F32)


def _cmpab_body(pt_ref, *refs, pps):
    del pt_ref
    kp, vp = refs[:pps], refs[pps:2 * pps]
    wk_ref, wv_ref, abk_ref, abv_ref = refs[2 * pps:]
    nsub = PAGE // CMP_STRIDE
    _cmp_accumulate(
        lambda l: jnp.concatenate([r[0, pl.ds(l, nsub, stride=CMP_STRIDE), :] for r in kp], axis=0),
        lambda l: jnp.concatenate([r[0, pl.ds(l, nsub, stride=CMP_STRIDE), :] for r in vp], axis=0),
        wk_ref, wv_ref, abk_ref, abv_ref)


def _cmpabt_body(pt_ref, *refs, pps):
    del pt_ref
    pages = refs[:pps]
    wk_ref, wv_ref, abk_ref, abv_ref, xk_ref, xv_ref = refs[pps:]
    for k, r in enumerate(pages):
        xk_ref[PAGE * k: PAGE * (k + 1), :] = r[0, 0].reshape(2 * HEAD_DIM, PAGE).T
        xv_ref[PAGE * k: PAGE * (k + 1), :] = r[0, 1].reshape(2 * HEAD_DIM, PAGE).T
    nsub = pps * PAGE // CMP_STRIDE
    _cmp_accumulate(lambda l: xk_ref[pl.ds(l, nsub, stride=CMP_STRIDE), :],
                    lambda l: xv_ref[pl.ds(l, nsub, stride=CMP_STRIDE), :], wk_ref, wv_ref, abk_ref, abv_ref)


def _cmp_ab_t(pool_t, pt_flat, wk, wv, pps):
    npg = pt_flat.shape[0]
    rows = pps * PAGE // CMP_STRIDE
    pspec = [pl.BlockSpec((1, 2, 2, HEAD_DIM, PAGE), lambda s, pt, k=k: (pt[s * pps + k], 0, 0, 0, 0)) for k in range(pps)]
    wspec = pl.BlockSpec((CMP_STRIDE * LANES, 512), lambda s, pt: (0, 0), pipeline_mode=pl.Buffered(1))
    ospec = pl.BlockSpec((rows, 512), lambda s, pt: (s, 0))
    gs = pltpu.PrefetchScalarGridSpec(num_scalar_prefetch=1, grid=(npg // pps,), in_specs=pspec + [wspec, wspec],
                                      out_specs=[ospec, ospec],
                                      scratch_shapes=[pltpu.VMEM((pps * PAGE, LANES), F32), pltpu.VMEM((pps * PAGE, LANES), F32)])
    shp = jax.ShapeDtypeStruct((npg * PAGE // CMP_STRIDE, 512), F32)
    return pl.pallas_call(functools.partial(_cmpabt_body, pps=pps), grid_spec=gs, out_shape=[shp, shp],
                          compiler_params=_cparams("parallel"), name="cmp_ab_t")(pt_flat, *([pool_t] * pps), wk, wv)


def _cmp_ab(pool, pt_flat, wk, wv, pps):
    npg = pt_flat.shape[0]
    nsub = PAGE // CMP_STRIDE
    rows = pps * nsub
    kspec = [pl.BlockSpec((1, PAGE, LANES), lambda s, pt, k=k: (pt[s * pps + k], 0, 0)) for k in range(pps)]
    vspec = [pl.BlockSpec((1, PAGE, LANES), lambda s, pt, k=k: (pt[s * pps + k], 0, 1)) for k in range(pps)]
    wspec = pl.BlockSpec((CMP_STRIDE * LANES, 512), lambda s, pt: (0, 0), pipeline_mode=pl.Buffered(1))
    ospec = pl.BlockSpec((rows, 512), lambda s, pt: (s, 0))
    gs = pltpu.PrefetchScalarGridSpec(num_scalar_prefetch=1, grid=(npg // pps,),
                                      in_specs=kspec + vspec + [wspec, wspec], out_specs=[ospec, ospec])
    shp = jax.ShapeDtypeStruct((npg * nsub, 512), F32)
    return pl.pallas_call(functools.partial(_cmpab_body, pps=pps), grid_spec=gs, out_shape=[shp, shp],
                          compiler_params=_cparams("parallel"), name="cmp_ab")(
        pt_flat, *([pool] * (2 * pps)), wk, wv)


def _gelu_tanh(x):
    return 0.5 * x * (1.0 + jnp.tanh(math.sqrt(2.0 / math.pi) * (x + 0.044715 * (x * x * x))))


def _cmpfin_body(abk_ref, abv_ref, pek_ref, w1k_ref, w2k_ref, pev_ref, w1v_ref, w2v_ref, seg_ref, gk_ref, c_ref, s_ref,
                 kc_ref, vc_ref, *, nc):
    lane = lax.broadcasted_iota(jnp.int32, (nc, LANES), 1)
    first8 = (lane % HEAD_DIM) < (ROT_DIM // 2)
    seg = seg_ref[...]
    cosv = c_ref[...]
    sinv = s_ref[...]

    def summaries(ab_ref, pe_ref, w1_ref, w2_ref, g):
        bias = jnp.dot(pe_ref[...].astype(BF), w1_ref[...].astype(BF), preferred_element_type=F32)[0:1, :]
        a = ab_ref[:, 256 * g: 256 * g + LANES]
        b = ab_ref[:, 256 * g + LANES: 256 * g + 2 * LANES]
        hid = _gelu_tanh(a + pltpu.roll(b, nc - 1, 0) + bias)
        return jnp.dot(hid.astype(BF), w2_ref[...], preferred_element_type=F32)

    for g in range(NSA_KV_HEADS):
        k = summaries(abk_ref, pek_ref, w1k_ref, w2k_ref, g)
        zz = k * k
        hi = zz.astype(BF)
        lo = (zz - hi.astype(F32)).astype(BF)
        ms = jnp.dot(hi, seg, preferred_element_type=F32) + jnp.dot(lo, seg, preferred_element_type=F32)
        k = k * lax.rsqrt(ms + EPS) * gk_ref[...]
        partner = jnp.where(first8, pltpu.roll(k, LANES - ROT_DIM // 2, 1), pltpu.roll(k, ROT_DIM // 2, 1))
        k = k * cosv + partner * sinv
        kc_ref[0, :, LANES * g: LANES * (g + 1)] = (k * SCALE).astype(BF)
        vc_ref[0, :, LANES * g: LANES * (g + 1)] = summaries(abv_ref, pev_ref, w1v_ref, w2v_ref, g).astype(BF)


def _cmp_fin(abk, abv, pek, w1k, w2k, pev, w1v, w2v, seg, gk, cosc, sinc, nseq, nc):
    ab = pl.BlockSpec((nc, 512), lambda b: (b, 0))
    kdim = CMP_BLOCK * HEAD_DIM
    out = pl.BlockSpec((1, nc, 256), lambda b: (b, 0, 0))
    shp = jax.ShapeDtypeStruct((nseq, nc, 256), BF)
    return pl.pallas_call(
        functools.partial(_cmpfin_body, nc=nc), grid=(nseq,),
        in_specs=[ab, ab, _const_spec((8, kdim)), _const_spec((kdim, CMP_HIDDEN)), _const_spec((CMP_HIDDEN, LANES)),
                  _const_spec((8, kdim)), _const_spec((kdim, CMP_HIDDEN)), _const_spec((CMP_HIDDEN, LANES)),
                  _const_spec((LANES, LANES)), _const_spec((1, LANES)), _const_spec((nc, LANES)), _const_spec((nc, LANES))],
        out_specs=[out, out], out_shape=[shp, shp],
        compiler_params=_cparams("parallel"), name="cmp_fin")(abk, abv, pek, w1k, w2k, pev, w1v, w2v, seg, gk, cosc, sinc)


def _cmpattn_body(seq_ref, pos_ref, q_ref, kc_ref, vc_ref, m_ref, oc_ref, sb_ref, *, nq, nc, nl, nblk, k_pick):
    del seq_ref
    pos0 = pos_ref[pl.program_id(0)]
    q = q_ref[0]
    lane = lax.broadcasted_iota(jnp.int32, (nq, LANES), 1)
    low = lane < HEAD_DIM
    trow = pos0 + lax.broadcasted_iota(jnp.int32, (nq, 1), 0)
    t4 = jnp.concatenate([trow] * 4, axis=0)
    posc = CMP_STRIDE * lax.broadcasted_iota(jnp.int32, (1, nc), 1) + (CMP_BLOCK - 1)
    valid = posc <= t4
    imps = []
    for g in range(NSA_KV_HEADS):
        rows = []
        for hh in range(4):
            h = 4 * g + hh
            pr = q[:, LANES * (h // 2): LANES * (h // 2 + 1)]
            rows.append(jnp.where(low, pr, 0.0) if h % 2 == 0 else jnp.where(low, 0.0, pr))
        qg = (jnp.concatenate(rows, axis=0) * LOG2E).astype(BF)
        s = lax.dot_general(qg, kc_ref[0, :, LANES * g: LANES * (g + 1)], NT, preferred_element_type=F32)
        s = jnp.where(valid, s, NEG)
        m = jnp.max(s, axis=1, keepdims=True)
        e = jnp.exp2(s - m)
        p = e * jnp.where(m > 0.5 * NEG, 1.0 / jnp.sum(e, axis=1, keepdims=True), 0.0)
        o = jnp.dot(p.astype(BF), vc_ref[0, :, LANES * g: LANES * (g + 1)], preferred_element_type=F32)
        for pj in range(2):
            oc_ref[0, :, 256 * g + LANES * pj: 256 * g + LANES * (pj + 1)] = jnp.where(
                low, o[2 * pj * nq: (2 * pj + 1) * nq], o[(2 * pj + 1) * nq: (2 * pj + 2) * nq])
        imps.append(p[0:nq] + p[nq:2 * nq] + p[2 * nq:3 * nq] + p[3 * nq:4 * nq])

    nr = max(2 * nq, LANES)
    imp = jnp.concatenate(imps + ([jnp.zeros((nr - 2 * nq, nc), F32)] if nr > 2 * nq else []), axis=0)
    mt = m_ref[...]
    ps = sum(lax.dot_general(mt, part, NT, preferred_element_type=F32) for part in _split3(imp))
    blk = lax.broadcasted_iota(jnp.int32, (nl, 1), 0)
    blkf = blk.astype(F32)
    tcol = pos0 + lax.broadcasted_iota(jnp.int32, (1, nr), 1) % nq
    cur = tcol // SEL_BLOCK
    force = (blk == 0) | (blk == cur) | (blk == cur - 1)
    score = jnp.where(force, FORCE_SCORE, jnp.where(blk * SEL_BLOCK <= tcol, ps, -1.0))
    score = jnp.where(blk < nblk, score, -1.0)

    def pick(_, sc):
        mx = jnp.max(sc, axis=0, keepdims=True)
        idx = jnp.min(jnp.where(sc == mx, blkf, 1e9), axis=0, keepdims=True)
        return jnp.where(blkf == idx, -2.0, sc)

    picked = lax.fori_loop(0, k_pick, pick, score) == -2.0
    sel_rows = jnp.where(picked, 0.0, SEL_OFF).T
    for g in range(NSA_KV_HEADS):
        sb_ref[0, :, nl * g: nl * (g + 1)] = sel_rows[nq * g: nq * (g + 1)].astype(BF)


def _cmp_attn(q3, kcd, vcd, mmat, seq_of, pos0, nq, nblk, k_pick):
    nb = q3.shape[0]
    nc = kcd.shape[1]
    nl = mmat.shape[0]
    gs = pltpu.PrefetchScalarGridSpec(
        num_scalar_prefetch=2, grid=(nb,),
        in_specs=[pl.BlockSpec((1, nq, 512), lambda b, sq, ps: (b, 0, 0)),
                  pl.BlockSpec((1, nc, 256), lambda b, sq, ps: (sq[b], 0, 0)),
                  pl.BlockSpec((1, nc, 256), lambda b, sq, ps: (sq[b], 0, 0)),
                  pl.BlockSpec((nl, nc), lambda b, sq, ps: (0, 0), pipeline_mode=pl.Buffered(1))],
        out_specs=[pl.BlockSpec((1, nq, 512), lambda b, sq, ps: (b, 0, 0)),
                   pl.BlockSpec((1, nq, 2 * nl), lambda b, sq, ps: (b, 0, 0))])
    return pl.pallas_call(
        functools.partial(_cmpattn_body, nq=nq, nc=nc, nl=nl, nblk=nblk, k_pick=k_pick), grid_spec=gs,
        out_shape=[jax.ShapeDtypeStruct((nb, nq, 512), F32), jax.ShapeDtypeStruct((nb, nq, 2 * nl), BF)],
        compiler_params=_cparams("parallel"), name="cmp_attn")(seq_of, pos0, q3, kcd, vcd, mmat)


def _group_rows(q, low, width_tail=None):
    rows = []
    for hh in range(4):
        pr = q[:, LANES * (hh // 2): LANES * (hh // 2 + 1)]
        rows.append(jnp.where(low, pr, 0.0) if hh % 2 == 0 else jnp.where(low, 0.0, pr))
    return rows


def _store_group(o_ref, o, low, n):
    for pj in range(2):
        o_ref[:, LANES * pj: LANES * (pj + 1)] = jnp.where(low, o[2 * pj * n: (2 * pj + 1) * n],
                                                           o[(2 * pj + 1) * n: (2 * pj + 2) * n])


def _selp_body(qi_ref, kj_ref, q_ref, sb_ref, ka_ref, v_ref, o_ref, qm_ref, m_ref, l_ref, acc_ref, *, tq, rb):
    sid = pl.program_id(1)
    i = qi_ref[sid]
    j = kj_ref[sid]
    low = lax.broadcasted_iota(jnp.int32, (tq, LANES), 1) < HEAD_DIM

    @pl.when(j == 0)
    def _():
        q = q_ref[...] * LOG2E
        for hh in range(4):
            pr = q[:, LANES * (hh // 2): LANES * (hh // 2 + 1)]
            sw = pltpu.roll(pr, HEAD_DIM, 1)
            qm_ref[hh] = (jnp.where(low, pr, sw) if hh % 2 == 0 else jnp.where(low, sw, pr)).astype(BF)
        m_ref[...] = jnp.full_like(m_ref, NEG)
        l_ref[...] = jnp.zeros_like(l_ref)
        acc_ref[...] = jnp.zeros_like(acc_ref)

    bias_half = (lax.broadcasted_iota(jnp.int32, (tq, LANES), 1) // HEAD_DIM) == ((j * tq) // (HEAD_DIM * SEL_BLOCK)) % 2

    def step(diag):
        for hh in range(4):
            for t0 in range(0, tq, rb):
                nk = t0 + rb if diag else tq
                qa = jnp.where(bias_half[t0:t0 + rb], sb_ref[t0:t0 + rb, :], qm_ref[hh, t0:t0 + rb, :])
                s = lax.dot_general(qa, ka_ref[0, 0:nk, :], NT, preferred_element_type=F32)
                if diag:
                    s = _causal_chunk(s, t0)
                _chunk_update(s, v_ref[0, 0:nk, :], m_ref, l_ref, acc_ref, 0, hh * tq + t0, rb)

    pl.when(j < i)(lambda: step(False))

    @pl.when(j == i)
    def _():
        step(True)
        _store_group(o_ref, acc_ref[0] / l_ref[0], low, tq)


def _sel_prompt(qn, selb, kaug, vdup, t, tq, rb, nl):
    n = t // tq
    qi, kj = _tri_steps(n)
    nsup = nl // LANES
    per_sup = LANES * SEL_BLOCK // tq
    gs = pltpu.PrefetchScalarGridSpec(
        num_scalar_prefetch=2, grid=(NSA_KV_HEADS, qi.shape[0]),
        in_specs=[pl.BlockSpec((tq, 256), lambda g, s, qi, kj: (qi[s], g)),
                  pl.BlockSpec((tq, LANES), lambda g, s, qi, kj: (qi[s], g * nsup + kj[s] // per_sup)),
                  pl.BlockSpec((1, tq, LANES), lambda g, s, qi, kj: (g, kj[s], 0)),
                  pl.BlockSpec((1, tq, LANES), lambda g, s, qi, kj: (g, kj[s], 0))],
        out_specs=pl.BlockSpec((tq, 256), lambda g, s, qi, kj: (qi[s], g)),
        scratch_shapes=[pltpu.VMEM((4, tq, LANES), BF), pltpu.VMEM((1, 4 * tq, LANES), F32),
                        pltpu.VMEM((1, 4 * tq, LANES), F32), pltpu.VMEM((1, 4 * tq, LANES), F32)])
    return pl.pallas_call(
        functools.partial(_selp_body, tq=tq, rb=rb), grid_spec=gs,
        out_shape=jax.ShapeDtypeStruct((t, 512), F32),
        compiler_params=_cparams("parallel", "arbitrary"), name="sel_prompt")(qi, kj, qn, selb, kaug, vdup)


def _winp_body(q_ref, *refs, tq, nprev):
    nt = nprev + 1
    k_refs, v_refs, o_ref = refs[:nt], refs[nt:2 * nt], refs[2 * nt]
    i = pl.program_id(1)
    low = lax.broadcasted_iota(jnp.int32, (tq, LANES), 1) < HEAD_DIM
    k = jnp.concatenate([r[0] for r in k_refs], axis=0)
    v = jnp.concatenate([r[0] for r in v_refs], axis=0)
    nk = k.shape[0]
    tpos = i * tq + lax.broadcasted_iota(jnp.int32, (tq, nk), 0)
    kpos = (i - nprev) * tq + lax.broadcasted_iota(jnp.int32, (tq, nk), 1)
    ok = (kpos <= tpos) & (tpos - kpos < WINDOW) & (kpos >= 0)
    outs = []
    for r in _group_rows(q_ref[...] * LOG2E, low):
        s = jnp.where(ok, lax.dot_general(r.astype(BF), k, NT, preferred_element_type=F32), NEG)
        e = jnp.exp2(s - jnp.max(s, axis=1, keepdims=True))
        outs.append(jnp.dot(e.astype(BF), v, preferred_element_type=F32) / jnp.sum(e, axis=1, keepdims=True))
    _store_group(o_ref, jnp.concatenate(outs, axis=0), low, tq)


def _win_prompt(qn, kwdup, vwdup, t, tq):
    nprev = WINDOW // tq
    kv = [pl.BlockSpec((1, tq, LANES), lambda g, i, dd=dd: (g, jnp.maximum(i - nprev + dd, 0), 0)) for dd in range(nprev + 1)]
    return pl.pallas_call(
        functools.partial(_winp_body, tq=tq, nprev=nprev), grid=(NSA_KV_HEADS, t // tq),
        in_specs=[pl.BlockSpec((tq, 256), lambda g, i: (i, g))] + kv + kv,
        out_specs=pl.BlockSpec((tq, 256), lambda g, i: (i, g)),
        out_shape=jax.ShapeDtypeStruct((t, 512), F32),
        compiler_params=_cparams("parallel", "parallel"), name="win_prompt")(
        qn, *([kwdup] * (nprev + 1)), *([vwdup] * (nprev + 1)))


ROWS_S = NSA_HEADS * NQS
ROWS_G = ROWS_S // NSA_KV_HEADS


def _new_key_mask(shape):
    r = lax.broadcasted_iota(jnp.int32, shape, 0) % NQS
    c = lax.broadcasted_iota(jnp.int32, shape, 1)
    return c <= r


def _slab_update(s, pv_of, m_ref, l_ref, acc_ref):
    m_prev = m_ref[0]
    m_new = jnp.maximum(m_prev, jnp.max(s, axis=1, keepdims=True))
    alpha = jnp.exp(m_prev - m_new)
    p = jnp.exp(s - m_new)
    l_ref[0] = alpha * l_ref[0] + jnp.sum(p, axis=1, keepdims=True)
    acc_ref[0] = alpha * acc_ref[0] + pv_of(p)
    m_ref[0] = m_new


def _slab_finish(sn, pv_of, m_ref, l_ref, acc_ref):
    m_prev = m_ref[0]
    m_new = jnp.maximum(m_prev, jnp.max(sn, axis=1, keepdims=True))
    alpha = jnp.exp(m_prev - m_new)
    pn = jnp.exp(sn - m_new)
    l = alpha * l_ref[0] + jnp.sum(pn, axis=1, keepdims=True)
    return (alpha * acc_ref[0] + pv_of(pn)) / l


def _slab_init(m_ref, l_ref, acc_ref):
    m_ref[...] = jnp.full_like(m_ref, NEG)
    l_ref[...] = jnp.zeros_like(l_ref)
    acc_ref[...] = jnp.zeros_like(acc_ref)


def _grp(a, g):
    return a[ROWS_G * g: ROWS_G * (g + 1)]


def _sels_body(pt_ref, q_ref, sb0_ref, sb1_ref, e_ref, kn_ref, vn_ref, *refs, pps, nsteps, per_chunk):
    del pt_ref
    pages = refs[:pps]
    o_ref, m_ref, l_ref, acc_ref = refs[pps:]
    sid = pl.program_id(1)
    pl.when(sid == 0)(lambda: _slab_init(m_ref, l_ref, acc_ref))

    q = q_ref[0] * SCALE
    qb = q.astype(BF)
    slab = lambda kind, g: jnp.concatenate([r[0, kind, g] for r in pages], axis=1).astype(BF)
    sbrows = jnp.concatenate([sb0_ref[0].astype(F32)] * 4 + [sb1_ref[0].astype(F32)] * 4, axis=0).astype(BF)
    bias = jnp.dot(sbrows, e_ref[sid % per_chunk], preferred_element_type=F32)
    s = jnp.concatenate([jnp.dot(_grp(qb, g), slab(0, g), preferred_element_type=F32)
                         for g in range(NSA_KV_HEADS)], axis=0) + bias
    _slab_update(s, lambda p: jnp.concatenate(
        [lax.dot_general(_grp(p, g).astype(BF), slab(1, g), NT, preferred_element_type=F32)
         for g in range(NSA_KV_HEADS)], axis=0), m_ref, l_ref, acc_ref)

    @pl.when(sid == nsteps - 1)
    def _():
        sn = jnp.concatenate([lax.dot_general(_grp(q, g), kn_ref[0, g], NT, preferred_element_type=F32)
                              for g in range(NSA_KV_HEADS)], axis=0)
        sn = jnp.where(_new_key_mask(sn.shape), sn, NEG)
        o_ref[0] = _slab_finish(sn, lambda pn: jnp.concatenate(
            [jnp.dot(_grp(pn, g), vn_ref[0, g], preferred_element_type=F32) for g in range(NSA_KV_HEADS)], axis=0),
            m_ref, l_ref, acc_ref)


def _sel_sample(qh, selb, kn, vn, pool_t, pt_flat, eexp, b, npg, pps, nl):
    nsteps = npg // pps
    per_chunk = eexp.shape[0]
    nchunk = nl // LANES
    pg = [pl.BlockSpec((1, 2, 2, HEAD_DIM, PAGE), lambda bb, s, pt, k=k: (pt[bb * npg + s * pps + k], 1, 0, 0, 0))
          for k in range(pps)]
    row = pl.BlockSpec((1, ROWS_S, HEAD_DIM), lambda bb, s, pt: (bb, 0, 0))
    new = pl.BlockSpec((1, NSA_KV_HEADS, NQS, HEAD_DIM), lambda bb, s, pt: (bb, 0, 0, 0))
    gs = pltpu.PrefetchScalarGridSpec(
        num_scalar_prefetch=1, grid=(b, nsteps),
        in_specs=[row,
                  pl.BlockSpec((1, NQS, LANES), lambda bb, s, pt: (bb, 0, s // per_chunk)),
                  pl.BlockSpec((1, NQS, LANES), lambda bb, s, pt: (bb, 0, nchunk + s // per_chunk)),
                  pl.BlockSpec(eexp.shape, lambda bb, s, pt: (0, 0, 0), pipeline_mode=pl.Buffered(1)),
                  new, new] + pg,
        out_specs=row,
        scratch_shapes=[pltpu.VMEM((1, ROWS_S, 1), F32), pltpu.VMEM((1, ROWS_S, 1), F32),
                        pltpu.VMEM((1, ROWS_S, HEAD_DIM), F32)])
    return pl.pallas_call(
        functools.partial(_sels_body, pps=pps, nsteps=nsteps, per_chunk=per_chunk), grid_spec=gs,
        out_shape=jax.ShapeDtypeStruct((b, ROWS_S, HEAD_DIM), F32),
        compiler_params=_cparams("parallel", "arbitrary"), name="sel_sample")(
        pt_flat, qh, selb, selb, eexp, kn, vn, *([pool_t] * pps))


def _wins_body(q_ref, st_ref, kn_ref, vn_ref, o_ref, *, wb):
    q = q_ref[0] * SCALE
    qb = q.astype(BF)
    s = jnp.concatenate([jnp.dot(_grp(qb, g), st_ref[0, 0, g].astype(BF), preferred_element_type=F32)
                         for g in range(NSA_KV_HEADS)], axis=0)
    tq = lax.broadcasted_iota(jnp.int32, s.shape, 0) % NQS
    n = lax.broadcasted_iota(jnp.int32, s.shape, 1)
    s = jnp.where(n > tq + (wb - WINDOW), s, NEG)
    sn = jnp.concatenate([lax.dot_general(_grp(q, g), kn_ref[0, g], NT, preferred_element_type=F32)
                          for g in range(NSA_KV_HEADS)], axis=0)
    sn = jnp.where(_new_key_mask(sn.shape), sn, NEG)
    m = jnp.maximum(jnp.max(s, axis=1, keepdims=True), jnp.max(sn, axis=1, keepdims=True))
    e = jnp.exp(s - m)
    en = jnp.exp(sn - m)
    den = jnp.sum(e, axis=1, keepdims=True) + jnp.sum(en, axis=1, keepdims=True)
    o = jnp.concatenate(
        [lax.dot_general(_grp(e, g).astype(BF), st_ref[0, 1, g].astype(BF), NT, preferred_element_type=F32)
         + jnp.dot(_grp(en, g), vn_ref[0, g], preferred_element_type=F32) for g in range(NSA_KV_HEADS)], axis=0)
    o_ref[0] = o / den


def _win_sample(qh, state_t, kn, vn, b, wb):
    row = pl.BlockSpec((1, ROWS_S, HEAD_DIM), lambda bb: (bb, 0, 0))
    new = pl.BlockSpec((1, NSA_KV_HEADS, NQS, HEAD_DIM), lambda bb: (bb, 0, 0, 0))
    return pl.pallas_call(
        functools.partial(_wins_body, wb=wb), grid=(b,),
        in_specs=[row, pl.BlockSpec((1, 2, NSA_KV_HEADS, HEAD_DIM, wb), lambda bb: (bb, 0, 0, 0, 0)), new, new],
        out_specs=row, out_shape=jax.ShapeDtypeStruct((b, ROWS_S, HEAD_DIM), F32),
        compiler_params=_cparams("parallel"), name="win_sample")(qh, state_t, kn, vn)


def _foxs_body(pt_ref, q_ref, kn_ref, vn_ref, lfn_ref, u_ref, *refs, pps, nsteps):
    del pt_ref
    kv = refs[:pps]
    lfp = refs[pps:2 * pps]
    o_ref, m_ref, l_ref, acc_ref, carry_ref = refs[2 * pps:]
    sid = pl.program_id(1)

    @pl.when(sid == 0)
    def _():
        _slab_init(m_ref, l_ref, acc_ref)
        carry_ref[...] = jnp.zeros_like(carry_ref)

    q = q_ref[0] * SCALE
    qb = q.astype(BF)
    head = lambda a, h: a[NQS * h: NQS * (h + 1)]
    keys = u_ref.shape[0]
    cpp = keys // PAGE
    for c0 in range(0, pps, cpp):
        kvc, lfc = kv[c0:c0 + cpp], lfp[c0:c0 + cpp]
        slab = lambda kind, h: jnp.concatenate([r[0, kind, h] for r in kvc], axis=1).astype(BF)
        lft = jnp.concatenate([r[0] for r in lfc], axis=1)
        cum = _dot3(lft, u_ref[...]) + carry_ref[:, 0:1]
        carry_ref[...] = jnp.broadcast_to(cum[:, keys - 1:], carry_ref.shape)
        s = jnp.concatenate([jnp.dot(head(qb, h), slab(0, h), preferred_element_type=F32)
                             - jnp.broadcast_to(cum[h:h + 1, :], (NQS, keys)) for h in range(FOX_HEADS)], axis=0)
        _slab_update(s, lambda p: jnp.concatenate(
            [lax.dot_general(head(p, h).astype(BF), slab(1, h), NT, preferred_element_type=F32)
             for h in range(FOX_HEADS)], axis=0), m_ref, l_ref, acc_ref)

    @pl.when(sid == nsteps - 1)
    def _():
        lfn = lfn_ref[0]
        col = lax.broadcasted_iota(jnp.int32, (FOX_HEADS, NQS), 1)
        cn = carry_ref[:, 0:NQS]
        for i in range(NQS):
            cn = cn + jnp.where(col >= i, lfn[:, i:i + 1], 0.0)
        sn = jnp.concatenate([lax.dot_general(head(q, h), kn_ref[0, h], NT, preferred_element_type=F32)
                              - jnp.broadcast_to(cn[h:h + 1, :], (NQS, NQS)) for h in range(FOX_HEADS)], axis=0)
        sn = jnp.where(_new_key_mask(sn.shape), sn, NEG)
        o_ref[0] = _slab_finish(sn, lambda pn: jnp.concatenate(
            [jnp.dot(head(pn, h), vn_ref[0, h], preferred_element_type=F32) for h in range(FOX_HEADS)], axis=0),
            m_ref, l_ref, acc_ref)


def _fox_sample(qh, kn, vn, lfn, pool_t, lf_t, pt_flat, b, npg, pps):
    nsteps = npg // pps
    keys = math.gcd(pps, 8) * PAGE
    u = jnp.asarray(np.triu(np.ones((keys, keys), np.float32)), BF)
    kvs = [pl.BlockSpec((1, 2, FOX_HEADS, HEAD_DIM, PAGE), lambda bb, s, pt, k=k: (pt[bb * npg + s * pps + k], 0, 0, 0, 0))
           for k in range(pps)]
    lfs = [pl.BlockSpec((1, FOX_HEADS, PAGE), lambda bb, s, pt, k=k: (pt[bb * npg + s * pps + k], 0, 0)) for k in range(pps)]
    row = pl.BlockSpec((1, ROWS_S, HEAD_DIM), lambda bb, s, pt: (bb, 0, 0))
    new = pl.BlockSpec((1, FOX_HEADS, NQS, HEAD_DIM), lambda bb, s, pt: (bb, 0, 0, 0))
    gs = pltpu.PrefetchScalarGridSpec(
        num_scalar_prefetch=1, grid=(b, nsteps),
        in_specs=[row, new, new, pl.BlockSpec((1, FOX_HEADS, NQS), lambda bb, s, pt: (bb, 0, 0)),
                  pl.BlockSpec((keys, keys), lambda bb, s, pt: (0, 0), pipeline_mode=pl.Buffered(1))] + kvs + lfs,
        out_specs=row,
        scratch_shapes=[pltpu.VMEM((1, ROWS_S, 1), F32), pltpu.VMEM((1, ROWS_S, 1), F32),
                        pltpu.VMEM((1, ROWS_S, HEAD_DIM), F32), pltpu.VMEM((FOX_HEADS, LANES), F32)])
    return pl.pallas_call(
        functools.partial(_foxs_body, pps=pps, nsteps=nsteps), grid_spec=gs,
        out_shape=jax.ShapeDtypeStruct((b, ROWS_S, HEAD_DIM), F32),
        compiler_params=_cparams("parallel", "arbitrary"), name="fox_sample")(
        pt_flat, qh, kn, vn, lfn, u, *([pool_t] * pps), *([lf_t] * pps))


def _rope_tables(pos):
    half = ROT_DIM // 2
    inv = ROPE_THETA ** (-(jnp.arange(half, dtype=F32) * 2.0 / ROT_DIM))
    ang = pos.astype(F32)[:, None] * inv[None, :]
    cos, sin = jnp.cos(ang), jnp.sin(ang)
    n = pos.shape[0]
    c64 = jnp.concatenate([cos, cos, jnp.ones((n, HEAD_DIM - ROT_DIM), F32)], axis=1)
    s64 = jnp.concatenate([-sin, sin, jnp.zeros((n, HEAD_DIM - ROT_DIM), F32)], axis=1)
    return jnp.tile(c64, (1, 2)), jnp.tile(s64, (1, 2))


def _importance_matrix(nc, nl):
    m = np.zeros((nl, nc), np.float32)
    per = SEL_BLOCK // CMP_STRIDE
    for n in range(nc - 1):
        m[n // per, n] += 1.0
        m[(n + 1) // per, n] += 1.0
    return jnp.asarray(m, BF)


def _gate_expand():
    e = np.zeros((3, LANES, 512), np.float32)
    for i in range(3):
        for h in range(NSA_HEADS):
            e[i, 3 * h + i, HEAD_DIM * h: HEAD_DIM * (h + 1)] = 1.0
    return jnp.asarray(e, BF)


def _block_expand(pps):
    per_step = pps * PAGE // SEL_BLOCK
    per_chunk = LANES // per_step
    e = np.zeros((per_chunk, LANES, pps * PAGE), np.float32)
    for r in range(per_chunk):
        for key in range(pps * PAGE):
            e[r, r * per_step + key // SEL_BLOCK, key] = 1.0
    return jnp.asarray(e, BF)


def _cmp_weights(w1):
    top, bot = w1[:CMP_STRIDE], w1[CMP_STRIDE:]
    wl = jnp.concatenate([top, bot], axis=2)
    z = jnp.zeros_like(wl)
    w = jnp.concatenate([jnp.concatenate([wl, z], axis=2), jnp.concatenate([z, wl], axis=2)], axis=1)
    return w.reshape(CMP_STRIDE * LANES, 512).astype(BF)


def _pad_rows(a, rp):
    return jnp.pad(a, ((0, rp - a.shape[0]),) + ((0, 0),) * (a.ndim - 1))


def kernel(x_prompt, x_sample, cache_nsa_kv, cache_fox_kv, cache_fox_logf, state_nsa_win_kv, page_table, norm_ffn1, w1_gate, w1_up, w1_down, norm_mix, w_in, b_forget, nsa_q_norm, nsa_k_norm, cmp_pe_k, cmp_w1_k, cmp_w2_k, cmp_pe_v, cmp_w1_v, cmp_w2_v, fox_q_norm, fox_k_norm, w_branch_nsa, w_branch_fox, w_out, norm_ffn2, w2_gate, w2_up, w2_down):
    bp, t, d = x_prompt.shape
    b, ts, _ = x_sample.shape
    assert bp == 1 and norm_ffn1.shape[0] == 1 and ts <= 4
    npg = page_table.shape[1]
    past = npg * PAGE
    n_pool = cache_nsa_kv.shape[1]
    wb = state_nsa_win_kv.shape[2]
    tm = 256
    tq_fox = min(1024, t)
    tq_sel = min(1024, t)
    rb = 1024
    tq_win = 512
    nq_cmp = 128
    pps_cmp = math.gcd(math.gcd(t // PAGE, npg), 32)
    pps_sel, pps_fox = math.gcd(npg, 16), math.gcd(npg, 16)
    assert t % PAGE == 0 and t % tq_fox == 0 and t % tq_sel == 0 and t % tq_win == 0 and WINDOW % tq_win == 0
    assert (HEAD_DIM * SEL_BLOCK) % tq_sel == 0 and LANES % (2 * pps_sel) == 0

    rs = b * NQS
    tm_s = tm if rs >= tm else rs
    rsp = -(-rs // tm_s) * tm_s
    assert t % tm == 0
    xs = _pad_rows(jnp.pad(x_sample, ((0, 0), (0, NQS - ts), (0, 0))).reshape(rs, d), rsp)
    pos_s = jnp.concatenate([past + (jnp.arange(rs) % NQS), jnp.zeros((rsp - rs,), jnp.int32)])

    f = w1_gate.shape[2]
    fp = -(-f // LANES) * LANES
    padc = lambda w: jnp.pad(w.astype(BF), ((0, 0), (0, fp - f)))
    padr = lambda w: jnp.pad(w.astype(BF), ((0, fp - f), (0, 0)))
    wi = w_in[0]
    o2, o3, o4, o5 = 1280, 1304, 2840, 2848
    wa = jnp.concatenate([wi[:, :o2], wi[:, o3:o4], wi[:, o5:], wi[:, o2:o3], wi[:, o4:o5],
                          jnp.zeros((d, LANES - 32), F32)], axis=1).astype(BF)
    bm = jnp.zeros((1, LANES), F32).at[0, 24:32].set(b_forget[0])
    g2 = lambda g: jnp.tile(g, 2)[None, :]
    seg = jnp.asarray(np.kron(np.eye(2, dtype=np.float32), np.full((HEAD_DIM, HEAD_DIM), 1.0 / HEAD_DIM, np.float32)), BF)

    ffn1_w = (norm_ffn1, padc(w1_gate[0]), padc(w1_up[0]), padr(w1_down[0]))
    ffn2_w = (norm_ffn2, padc(w2_gate[0]), padc(w2_up[0]), padr(w2_down[0]))
    proj_w = (norm_mix, wa, seg, g2(nsa_q_norm[0]), g2(nsa_k_norm[0, 1]), g2(nsa_k_norm[0, 2]), g2(fox_q_norm[0]),
              g2(fox_k_norm[0]), bm)

    def front(x, pos, tmx):
        x1 = _ffn(x, *ffn1_w, tmx)
        return (x1,) + tuple(_proj(x1, *proj_w, *_rope_tables(pos), tmx))

    (x1, qn, nsa_rows, win_rows, qf, fox_rows, gm, misc, foxb, kaug, vdup, kwdup, vwdup) = front(
        x_prompt[0], jnp.arange(t), tm)
    (x1_s, qn_n, nsa_new, win_new_r, qf_n, fox_new, gm_s, misc_s) = front(xs, pos_s, tm_s)[:8]

    ct = _cumsum_t(misc, t, min(512, t)).reshape(FOX_HEADS // 2, 2, t)
    o_fox_p = _fox_prompt(qf, foxb, ct, t, tq_fox, rb)

    wk, wv = _cmp_weights(cmp_w1_k[0]), _cmp_weights(cmp_w1_v[0])
    kdim = CMP_BLOCK * HEAD_DIM
    pe8 = lambda pe: jnp.broadcast_to(pe.reshape(1, kdim), (8, kdim))
    w2d = lambda w2: jnp.concatenate([w2, w2], axis=1).astype(BF)
    gkc = g2(nsa_k_norm[0, 0])

    def compress(ab_fn, pool, pt_flat, nseq, nc):
        abk, abv = ab_fn(pool, pt_flat, wk, wv, pps_cmp)
        posc = jnp.arange(nc) * CMP_STRIDE + (CMP_BLOCK - 1)
        cosc, sinc = _rope_tables(posc)
        return _cmp_fin(abk, abv, pe8(cmp_pe_k[0]), cmp_w1_k[0].reshape(kdim, CMP_HIDDEN), w2d(cmp_w2_k[0]),
                        pe8(cmp_pe_v[0]), cmp_w1_v[0].reshape(kdim, CMP_HIDDEN), w2d(cmp_w2_v[0]), seg, gkc, cosc, sinc,
                        nseq, nc)

    nsub = PAGE // CMP_STRIDE
    tpg = t // PAGE
    kc_p, vc_p = compress(_cmp_ab, nsa_rows.reshape(tpg, PAGE, 512), jnp.arange(tpg, dtype=jnp.int32), 1, tpg * nsub)
    nsa_t = jnp.transpose(cache_nsa_kv[0], (0, 2, 3, 4, 1))
    fox_t = jnp.transpose(cache_fox_kv[0], (0, 2, 3, 4, 1))
    lf_t = jnp.transpose(cache_fox_logf[0], (0, 2, 1))
    st_t = jnp.transpose(state_nsa_win_kv[0], (0, 2, 3, 4, 1))
    pt_flat = page_table.reshape(-1).astype(jnp.int32)
    kc_s, vc_s = compress(_cmp_ab_t, nsa_t, pt_flat, b, npg * nsub)

    nl_p = -(-(t // SEL_BLOCK) // LANES) * LANES
    nl_s = -(-(past // SEL_BLOCK) // LANES) * LANES
    nbp = t // nq_cmp
    oc_p, sb_p = _cmp_attn(qn.reshape(nbp, nq_cmp, 512), kc_p, vc_p, _importance_matrix(tpg * nsub, nl_p),
                           jnp.zeros((nbp,), jnp.int32), jnp.arange(nbp, dtype=jnp.int32) * nq_cmp, nq_cmp,
                           t // SEL_BLOCK, min(N_SEL, t // SEL_BLOCK))
    new = lambda a: a[:rs]
    qn_s = new(qn_n).reshape(b, NQS, 512)
    oc_s, sb_s = _cmp_attn(qn_s, kc_s, vc_s, _importance_matrix(npg * nsub, nl_s), jnp.arange(b, dtype=jnp.int32),
                           jnp.full((b,), past, jnp.int32), NQS, past // SEL_BLOCK,
                           min(N_SEL, past // SEL_BLOCK + 1) - 1)

    os_p = _sel_prompt(qn, sb_p.reshape(t, 2 * nl_p), kaug, vdup, t, tq_sel, rb, nl_p)
    ow_p = _win_prompt(qn, kwdup, vwdup, t, tq_win)

    hm = lambda a, nh: new(a).reshape(b, NQS, nh, HEAD_DIM).transpose(0, 2, 1, 3)
    rows_of = lambda o: _pad_rows(o.reshape(b, NSA_HEADS, NQS, HEAD_DIM).transpose(0, 2, 1, 3).reshape(rs, 512), rsp)
    qn_h = hm(qn_n, NSA_HEADS).reshape(b, ROWS_S, HEAD_DIM)
    os_s = _sel_sample(qn_h, sb_s, hm(nsa_new[:, 256:384], NSA_KV_HEADS), hm(nsa_new[:, 384:512], NSA_KV_HEADS), nsa_t,
                       pt_flat, _block_expand(pps_sel), b, npg, pps_sel, nl_s)
    ow_s = _win_sample(qn_h, st_t, hm(win_new_r[:, 0:128], NSA_KV_HEADS), hm(win_new_r[:, 128:256], NSA_KV_HEADS), b, wb)
    of_s = _fox_sample(hm(qf_n, FOX_HEADS).reshape(b, ROWS_S, HEAD_DIM), hm(fox_new[:, 0:512], FOX_HEADS),
                       hm(fox_new[:, 512:1024], FOX_HEADS), new(misc_s)[:, 24:32].reshape(b, NQS, FOX_HEADS).transpose(0, 2, 1),
                       fox_t, lf_t, pt_flat, b, npg, pps_fox)

    merge_w = (_gate_expand(), w_branch_nsa[0].astype(BF), w_branch_fox[0].astype(BF), w_out[0].astype(BF))
    y = _ffn(_merge(x1, oc_p.reshape(t, 512), os_p, ow_p, o_fox_p, misc, gm, *merge_w, tm), *ffn2_w, tm)
    y_s = _ffn(_merge(x1_s, _pad_rows(oc_s.reshape(rs, 512), rsp), rows_of(os_s), rows_of(ow_s), rows_of(of_s), misc_s,
                      gm_s, *merge_w, tm_s), *ffn2_w, tm_s)

    smp = lambda a, *shape: new(a).reshape((b, NQS) + shape)[:, :ts]
    g, hd = NSA_KV_HEADS, HEAD_DIM
    win_new = smp(win_new_r, 2, g, hd)
    win_all_s = jnp.concatenate([state_nsa_win_kv[0], win_new], axis=1)
    keep_p = min(WINDOW, t)
    keep_s = min(WINDOW, wb + ts)
    return (y[None], smp(y_s, d),
            nsa_rows.reshape(1, 1, t, 4, g, hd), smp(nsa_new, 4, g, hd)[None],
            fox_rows.reshape(1, 1, t, 2, FOX_HEADS, hd), smp(fox_new, 2, FOX_HEADS, hd)[None],
            misc[:, 24:32].reshape(1, 1, t, FOX_HEADS), smp(misc_s[:, 24:32], FOX_HEADS)[None],
            win_rows[t - keep_p:].reshape(1, 1, keep_p, 2, g, hd), win_all_s[:, wb + ts - keep_s:][None])
```

```python
import functools
import math

import numpy as np
import jax
import jax.numpy as jnp
from jax import lax
from jax.experimental import pallas as pl
from jax.experimental.pallas import tpu as pltpu

F32 = jnp.float32
BF = jnp.bfloat16

HEAD_DIM = 64
NSA_HEADS = 8
NSA_KV_HEADS = 2
FOX_HEADS = 8
ROT_DIM = 16
ROPE_THETA = 500000.0
CMP_BLOCK = 32
CMP_STRIDE = 16
CMP_HIDDEN = 128
SEL_BLOCK = 64
N_SEL = 16
WINDOW = 512
PAGE = 128
EPS = 1e-6
FORCE_SCORE = 1e6
SCALE = 0.125
LOG2E = 1.4426950408889634

LANES = 128
NQS = 8
NEG = -1e30
SEL_OFF = -float(2 ** 30)
VMEM_LIMIT = 56 * 1024 * 1024

NT = (((1,), (1,)), ((), ()))


def _cparams(*sem):
    return pltpu.CompilerParams(dimension_semantics=sem, vmem_limit_bytes=VMEM_LIMIT)


def _const_spec(shape):
    return pl.BlockSpec(shape, lambda *_: (0,) * len(shape), pipeline_mode=pl.Buffered(1))


def _split3(a):
    hi = a.astype(BF)
    r1 = a - hi.astype(F32)
    mid = r1.astype(BF)
    lo = (r1 - mid.astype(F32)).astype(BF)
    return hi, mid, lo


def _dot3(a, m):
    hi, mid, lo = _split3(a)
    return (jnp.dot(hi, m, preferred_element_type=F32) + jnp.dot(mid, m, preferred_element_type=F32)
            + jnp.dot(lo, m, preferred_element_type=F32))


def _rms_rows(x, gain):
    return x * lax.rsqrt(jnp.mean(x * x, axis=-1, keepdims=True) + EPS) * gain


def _ffn_body(x_ref, n_ref, wg_ref, wu_ref, wd_ref, o_ref):
    x = x_ref[...]
    h = _rms_rows(x, n_ref[...]).astype(BF)
    g = jnp.dot(h, wg_ref[...], preferred_element_type=F32)
    u = jnp.dot(h, wu_ref[...], preferred_element_type=F32)
    a = (g * jax.nn.sigmoid(g) * u).astype(BF)
    o_ref[...] = x + 0.5 * jnp.dot(a, wd_ref[...], preferred_element_type=F32)


def _ffn(x, n, wg, wu, wd, tm):
    rp, d = x.shape
    f = wg.shape[1]
    row = pl.BlockSpec((tm, d), lambda i: (i, 0))
    return pl.pallas_call(
        _ffn_body, grid=(rp // tm,),
        in_specs=[row, _const_spec((1, d)), _const_spec((d, f)), _const_spec((d, f)), _const_spec((f, d))],
        out_specs=row, out_shape=jax.ShapeDtypeStruct((rp, d), F32),
        compiler_params=_cparams("parallel"), name="ffn")(x, n, wg, wu, wd)


def _proj_body(x_ref, nm_ref, w_ref, seg_ref, gq_ref, gks_ref, gkw_ref, gfq_ref, gfk_ref, bm_ref, c_ref, s_ref,
               qn_ref, nsa_ref, win_ref, qf_ref, fox_ref, gm_ref, misc_ref, foxb_ref, kaug_ref, vdup_ref, kwdup_ref,
               vwdup_ref, *, tm, d):
    x = x_ref[...]
    h = _rms_rows(x, nm_ref[...]).astype(BF)
    z = jnp.dot(h, w_ref[...], preferred_element_type=F32)
    cosv = c_ref[...]
    sinv = s_ref[...]
    seg = seg_ref[...]
    lane = lax.broadcasted_iota(jnp.int32, (tm, LANES), 1)
    first8 = (lane % HEAD_DIM) < (ROT_DIM // 2)
    low = lane < HEAD_DIM

    def hnorm(v, gain):
        zz = v * v
        hi = zz.astype(BF)
        lo = (zz - hi.astype(F32)).astype(BF)
        ms = jnp.dot(hi, seg, preferred_element_type=F32) + jnp.dot(lo, seg, preferred_element_type=F32)
        return v * lax.rsqrt(ms + EPS) * gain

    def rope(v):
        partner = jnp.where(first8, pltpu.roll(v, LANES - ROT_DIM // 2, 1), pltpu.roll(v, ROT_DIM // 2, 1))
        return v * cosv + partner * sinv

    def dup(v):
        r = pltpu.roll(v, HEAD_DIM, 1)
        return jnp.where(low, v, r), jnp.where(low, r, v)

    def chunk(off, c):
        return z[:, off + LANES * c: off + LANES * (c + 1)]

    o_q, o_kv, o_f, o_gm = 0, 512, 1280, 2816
    o_misc = o_gm + 2 * d
    for c in range(4):
        qn_ref[:, LANES * c: LANES * (c + 1)] = rope(hnorm(chunk(o_q, c), gq_ref[...]))
        qf_ref[:, LANES * c: LANES * (c + 1)] = hnorm(chunk(o_f, c), gfq_ref[...])
        fk = hnorm(chunk(o_f + 512, c), gfk_ref[...])
        fox_ref[:, LANES * c: LANES * (c + 1)] = fk
        foxb_ref[:, LANES * c: LANES * (c + 1)] = fk.astype(BF)
    fv = z[:, o_f + 1024: o_f + 1536]
    fox_ref[:, 512:1024] = fv
    foxb_ref[:, 512:1024] = fv.astype(BF)
    nsa_ref[:, 0:256] = z[:, o_kv: o_kv + 256]
    ks = rope(hnorm(chunk(o_kv, 2), gks_ref[...]))
    vs = chunk(o_kv, 3)
    kw = rope(hnorm(chunk(o_kv, 4), gkw_ref[...]))
    vw = chunk(o_kv, 5)
    nsa_ref[:, 256:384] = ks
    nsa_ref[:, 384:512] = vs
    win_ref[:, 0:128] = kw
    win_ref[:, 128:256] = vw
    gm_ref[...] = jax.nn.sigmoid(z[:, o_gm: o_gm + 2 * d])
    zm = z[:, o_misc: o_misc + LANES] + bm_ref[...]
    logsig = jnp.minimum(zm, 0.0) - jnp.log1p(jnp.exp(-jnp.abs(zm)))
    misc_ref[...] = jnp.where(lane < 3 * NSA_HEADS, jax.nn.sigmoid(zm), logsig)
    pos = pl.program_id(0) * tm + lax.broadcasted_iota(jnp.int32, (tm, 1), 0)
    blk = pos // SEL_BLOCK
    onehot = jnp.where(blk % LANES == lane, 1.0, 0.0)
    hot_half = (lane // HEAD_DIM) == (blk // HEAD_DIM) % 2
    k0, k1 = dup(ks * SCALE)
    v0, v1 = dup(vs)
    kw0, kw1 = dup(kw * SCALE)
    vw0, vw1 = dup(vw)
    for g, (kk, vv, kkw, vvw) in enumerate(((k0, v0, kw0, vw0), (k1, v1, kw1, vw1))):
        kaug_ref[g] = jnp.where(hot_half, onehot, kk).astype(BF)
        vdup_ref[g] = vv.astype(BF)
        kwdup_ref[g] = kkw.astype(BF)
        vwdup_ref[g] = vvw.astype(BF)


def _proj(x, nm, wa, seg, gq, gks, gkw, gfq, gfk, bm, cosv, sinv, tm):
    rp, d = x.shape
    n = wa.shape[1]
    row = lambda w: pl.BlockSpec((tm, w), lambda i: (i, 0))
    grp = lambda w: pl.BlockSpec((2, tm, w), lambda i: (0, i, 0))
    g128 = _const_spec((1, LANES))
    outs = [(512, F32), (512, F32), (256, F32), (512, F32), (1024, F32), (2 * d, F32), (LANES, F32), (1024, BF)]
    out_shape = [jax.ShapeDtypeStruct((rp, w), dt) for w, dt in outs]
    out_shape += [jax.ShapeDtypeStruct((2, rp, LANES), BF) for _ in range(4)]
    out_specs = [row(w) for w, _ in outs] + [grp(LANES) for _ in range(4)]
    return pl.pallas_call(
        functools.partial(_proj_body, tm=tm, d=d), grid=(rp // tm,),
        in_specs=[row(d), _const_spec((1, d)), _const_spec((d, n)), _const_spec((LANES, LANES)),
                  g128, g128, g128, g128, g128, g128, row(LANES), row(LANES)],
        out_specs=out_specs, out_shape=out_shape,
        compiler_params=_cparams("parallel"), name="proj")(x, nm, wa, seg, gq, gks, gkw, gfq, gfk, bm, cosv, sinv)


def _merge_body(x_ref, oc_ref, os_ref, ow_ref, of_ref, misc_ref, gm_ref, e_ref, wbn_ref, wbf_ref, wo_ref, o_ref, *, d):
    hi, mid, lo = _split3(misc_ref[...])

    def gate(i):
        e = e_ref[i]
        return (jnp.dot(hi, e, preferred_element_type=F32) + jnp.dot(mid, e, preferred_element_type=F32)
                + jnp.dot(lo, e, preferred_element_type=F32))

    o_nsa = gate(0) * oc_ref[...] + gate(1) * os_ref[...] + gate(2) * ow_ref[...]
    a = jnp.dot(o_nsa.astype(BF), wbn_ref[...], preferred_element_type=F32)
    b = jnp.dot(of_ref[...].astype(BF), wbf_ref[...], preferred_element_type=F32)
    gm = gm_ref[...]
    mixed = gm[:, :d] * a + gm[:, d:] * b
    o_ref[...] = x_ref[...] + jnp.dot(mixed.astype(BF), wo_ref[...], preferred_element_type=F32)


def _merge(x, oc, os_, ow, of, misc, gm, e, wbn, wbf, wo, tm):
    rp, d = x.shape
    row = lambda w: pl.BlockSpec((tm, w), lambda i: (i, 0))
    return pl.pallas_call(
        functools.partial(_merge_body, d=d), grid=(rp // tm,),
        in_specs=[row(d), row(512), row(512), row(512), row(512), row(LANES), row(2 * d),
                  _const_spec((3, LANES, 512)), _const_spec((512, d)), _const_spec((512, d)), _const_spec((d, d))],
        out_specs=row(d), out_shape=jax.ShapeDtypeStruct((rp, d), F32),
        compiler_params=_cparams("parallel"), name="merge")(x, oc, os_, ow, of, misc, gm, e, wbn, wbf, wo)


def _cumsum_body(m_ref, u_ref, o_ref, carry_ref, *, tb):
    @pl.when(pl.program_id(0) == 0)
    def _():
        carry_ref[...] = jnp.zeros_like(carry_ref)

    lft = m_ref[...].T
    a = lft[3 * NSA_HEADS: 3 * NSA_HEADS + FOX_HEADS, :]
    c = _dot3(a, u_ref[...]) + carry_ref[:, 0:1]
    o_ref[...] = c * LOG2E
    carry_ref[...] = jnp.broadcast_to(c[:, tb - 1: tb], carry_ref.shape)


def _cumsum_t(misc, t, tb):
    u = jnp.asarray(np.triu(np.ones((tb, tb), np.float32)), BF)
    return pl.pallas_call(
        functools.partial(_cumsum_body, tb=tb), grid=(t // tb,),
        in_specs=[pl.BlockSpec((tb, LANES), lambda i: (i, 0)), _const_spec((tb, tb))],
        out_specs=pl.BlockSpec((FOX_HEADS, tb), lambda i: (0, i)),
        out_shape=jax.ShapeDtypeStruct((FOX_HEADS, t), F32),
        scratch_shapes=[pltpu.VMEM((FOX_HEADS, LANES), F32)],
        compiler_params=_cparams("arbitrary"), name="fox_cumsum")(misc, u)


def _tri_steps(n):
    qi = np.concatenate([np.full(i + 1, i, np.int32) for i in range(n)])
    kj = np.concatenate([np.arange(i + 1, dtype=np.int32) for i in range(n)])
    return jnp.asarray(qi), jnp.asarray(kj)


def _chunk_update(s, v, m_ref, l_ref, acc_ref, idx, r0, rb):
    rows = slice(r0, r0 + rb)
    m_prev = m_ref[idx, rows, :]
    m_new = jnp.maximum(m_prev, jnp.max(s, axis=1, keepdims=True))
    alpha = jnp.exp2(m_prev - m_new)
    p = jnp.exp2(s - jnp.tile(m_new, (1, s.shape[1] // LANES)))
    l_ref[idx, rows, :] = alpha * l_ref[idx, rows, :] + jnp.sum(p, axis=1, keepdims=True)
    acc_ref[idx, rows, :] = alpha * acc_ref[idx, rows, :] + jnp.dot(p.astype(BF), v, preferred_element_type=F32)
    m_ref[idx, rows, :] = m_new


def _causal_chunk(s, r0):
    r = r0 + lax.broadcasted_iota(jnp.int32, s.shape, 0)
    c = lax.broadcasted_iota(jnp.int32, s.shape, 1)
    return jnp.where(c <= r, s, NEG)


def _foxp_body(qi_ref, kj_ref, q_ref, k_ref, v_ref, c_ref, o_ref, qm_ref, m_ref, l_ref, acc_ref, *, tb, rb):
    sid = pl.program_id(1)
    i = qi_ref[sid]
    j = kj_ref[sid]
    low = lax.broadcasted_iota(jnp.int32, (tb, LANES), 1) < HEAD_DIM

    @pl.when(j == 0)
    def _():
        q = q_ref[...] * (SCALE * LOG2E)
        qm_ref[0] = jnp.where(low, q, 0.0).astype(BF)
        qm_ref[1] = jnp.where(low, 0.0, q).astype(BF)
        m_ref[...] = jnp.full_like(m_ref, NEG)
        l_ref[...] = jnp.zeros_like(l_ref)
        acc_ref[...] = jnp.zeros_like(acc_ref)

    def step(diag):
        for hh in range(2):
            for r0 in range(0, tb, rb):
                nk = r0 + rb if diag else tb
                s = (lax.dot_general(qm_ref[hh, r0:r0 + rb, :], k_ref[0:nk, :], NT, preferred_element_type=F32)
                     - c_ref[0, hh:hh + 1, 0:nk])
                if diag:
                    s = _causal_chunk(s, r0)
                _chunk_update(s, v_ref[0:nk, :], m_ref, l_ref, acc_ref, hh, r0, rb)

    pl.when(j < i)(lambda: step(False))

    @pl.when(j == i)
    def _():
        step(True)
        o_ref[...] = jnp.where(low, acc_ref[0] / l_ref[0], acc_ref[1] / l_ref[1])


def _fox_prompt(qf, foxb, ct, t, tb, rb):
    n = t // tb
    qi, kj = _tri_steps(n)
    npair = FOX_HEADS // 2
    gs = pltpu.PrefetchScalarGridSpec(
        num_scalar_prefetch=2, grid=(npair, qi.shape[0]),
        in_specs=[pl.BlockSpec((tb, LANES), lambda p, s, qi, kj: (qi[s], p)),
                  pl.BlockSpec((tb, LANES), lambda p, s, qi, kj: (kj[s], p)),
                  pl.BlockSpec((tb, LANES), lambda p, s, qi, kj: (kj[s], npair + p)),
                  pl.BlockSpec((1, 2, tb), lambda p, s, qi, kj: (p, 0, kj[s]))],
        out_specs=pl.BlockSpec((tb, LANES), lambda p, s, qi, kj: (qi[s], p)),
        scratch_shapes=[pltpu.VMEM((2, tb, LANES), BF), pltpu.VMEM((2, tb, LANES), F32),
                        pltpu.VMEM((2, tb, LANES), F32), pltpu.VMEM((2, tb, LANES), F32)])
    return pl.pallas_call(
        functools.partial(_foxp_body, tb=tb, rb=rb), grid_spec=gs,
        out_shape=jax.ShapeDtypeStruct((t, 512), F32),
        compiler_params=_cparams("parallel", "arbitrary"), name="fox_prompt")(qi, kj, qf, foxb, foxb, ct)


def _cmp_accumulate(rows_k, rows_v, wk_ref, wv_ref, abk_ref, abv_ref):
    xk = jnp.concatenate([rows_k(l) for l in range(CMP_STRIDE)], axis=1).astype(BF)
    xv = jnp.concatenate([rows_v(l) for l in range(CMP_STRIDE)], axis=1).astype(BF)
    abk_ref[...] = jnp.dot(xk, wk_ref[...], preferred_element_type=F32)
    abv_ref[...] = jnp.dot(xv, wv_ref[...], preferred_element_type=F32)


def _cmpab_body(pt_ref, *refs, pps):
    del pt_ref
    kp, vp = refs[:pps], refs[pps:2 * pps]
    wk_ref, wv_ref, abk_ref, abv_ref = refs[2 * pps:]
    nsub = PAGE // CMP_STRIDE
    _cmp_accumulate(
        lambda l: jnp.concatenate([r[0, pl.ds(l, nsub, stride=CMP_STRIDE), :] for r in kp], axis=0),
        lambda l: jnp.concatenate([r[0, pl.ds(l, nsub, stride=CMP_STRIDE), :] for r in vp], axis=0),
        wk_ref, wv_ref, abk_ref, abv_ref)


def _cmpabt_body(pt_ref, *refs, pps):
    del pt_ref
    pages = refs[:pps]
    wk_ref, wv_ref, abk_ref, abv_ref, xk_ref, xv_ref = refs[pps:]
    for k, r in enumerate(pages):
        xk_ref[PAGE * k: PAGE * (k + 1), :] = r[0, 0].reshape(2 * HEAD_DIM, PAGE).T
        xv_ref[PAGE * k: PAGE * (k + 1), :] = r[0, 1].reshape(2 * HEAD_DIM, PAGE).T
    nsub = pps * PAGE // CMP_STRIDE
    _cmp_accumulate(lambda l: xk_ref[pl.ds(l, nsub, stride=CMP_STRIDE), :],
                    lambda l: xv_ref[pl.ds(l, nsub, stride=CMP_STRIDE), :], wk_ref, wv_ref, abk_ref, abv_ref)


def _cmp_ab_t(pool_t, pt_flat, wk, wv, pps):
    npg = pt_flat.shape[0]
    rows = pps * PAGE // CMP_STRIDE
    pspec = [pl.BlockSpec((1, 2, 2, HEAD_DIM, PAGE), lambda s, pt, k=k: (pt[s * pps + k], 0, 0, 0, 0)) for k in range(pps)]
    wspec = pl.BlockSpec((CMP_STRIDE * LANES, 512), lambda s, pt: (0, 0), pipeline_mode=pl.Buffered(1))
    ospec = pl.BlockSpec((rows, 512), lambda s, pt: (s, 0))
    gs = pltpu.PrefetchScalarGridSpec(num_scalar_prefetch=1, grid=(npg // pps,), in_specs=pspec + [wspec, wspec],
                                      out_specs=[ospec, ospec],
                                      scratch_shapes=[pltpu.VMEM((pps * PAGE, LANES), F32), pltpu.VMEM((pps * PAGE, LANES), F32)])
    shp = jax.ShapeDtypeStruct((npg * PAGE // CMP_STRIDE, 512), F32)
    return pl.pallas_call(functools.partial(_cmpabt_body, pps=pps), grid_spec=gs, out_shape=[shp, shp],
                          compiler_params=_cparams("parallel"), name="cmp_ab_t")(pt_flat, *([pool_t] * pps), wk, wv)


def _cmp_ab(pool, pt_flat, wk, wv, pps):
    npg = pt_flat.shape[0]
    nsub = PAGE // CMP_STRIDE
    rows = pps * nsub
    kspec = [pl.BlockSpec((1, PAGE, LANES), lambda s, pt, k=k: (pt[s * pps + k], 0, 0)) for k in range(pps)]
    vspec = [pl.BlockSpec((1, PAGE, LANES), lambda s, pt, k=k: (pt[s * pps + k], 0, 1)) for k in range(pps)]
    wspec = pl.BlockSpec((CMP_STRIDE * LANES, 512), lambda s, pt: (0, 0), pipeline_mode=pl.Buffered(1))
    ospec = pl.BlockSpec((rows, 512), lambda s, pt: (s, 0))
    gs = pltpu.PrefetchScalarGridSpec(num_scalar_prefetch=1, grid=(npg // pps,),
                                      in_specs=kspec + vspec + [wspec, wspec], out_specs=[ospec, ospec])
    shp = jax.ShapeDtypeStruct((npg * nsub, 512), F32)
    return pl.pallas_call(functools.partial(_cmpab_body, pps=pps), grid_spec=gs, out_shape=[shp, shp],
                          compiler_params=_cparams("parallel"), name="cmp_ab")(
        pt_flat, *([pool] * (2 * pps)), wk, wv)


def _gelu_tanh(x):
    return 0.5 * x * (1.0 + jnp.tanh(math.sqrt(2.0 / math.pi) * (x + 0.044715 * (x * x * x))))


def _cmpfin_body(abk_ref, abv_ref, pek_ref, w1k_ref, w2k_ref, pev_ref, w1v_ref, w2v_ref, seg_ref, gk_ref, c_ref, s_ref,
                 kc_ref, vc_ref, *, nc):
    lane = lax.broadcasted_iota(jnp.int32, (nc, LANES), 1)
    first8 = (lane % HEAD_DIM) < (ROT_DIM // 2)
    seg = seg_ref[...]
    cosv = c_ref[...]
    sinv = s_ref[...]

    def summaries(ab_ref, pe_ref, w1_ref, w2_ref, g):
        bias = jnp.dot(pe_ref[...].astype(BF), w1_ref[...].astype(BF), preferred_element_type=F32)[0:1, :]
        a = ab_ref[:, 256 * g: 256 * g + LANES]
        b = ab_ref[:, 256 * g + LANES: 256 * g + 2 * LANES]
        hid = _gelu_tanh(a + pltpu.roll(b, nc - 1, 0) + bias)
        return jnp.dot(hid.astype(BF), w2_ref[...], preferred_element_type=F32)

    for g in range(NSA_KV_HEADS):
        k = summaries(abk_ref, pek_ref, w1k_ref, w2k_ref, g)
        zz = k * k
        hi = zz.astype(BF)
        lo = (zz - hi.astype(F32)).astype(BF)
        ms = jnp.dot(hi, seg, preferred_element_type=F32) + jnp.dot(lo, seg, preferred_element_type=F32)
        k = k * lax.rsqrt(ms + EPS) * gk_ref[...]
        partner = jnp.where(first8, pltpu.roll(k, LANES - ROT_DIM // 2, 1), pltpu.roll(k, ROT_DIM // 2, 1))
        k = k * cosv + partner * sinv
        kc_ref[0, :, LANES * g: LANES * (g + 1)] = (k * SCALE).astype(BF)
        vc_ref[0, :, LANES * g: LANES * (g + 1)] = summaries(abv_ref, pev_ref, w1v_ref, w2v_ref, g).astype(BF)


def _cmp_fin(abk, abv, pek, w1k, w2k, pev, w1v, w2v, seg, gk, cosc, sinc, nseq, nc):
    ab = pl.BlockSpec((nc, 512), lambda b: (b, 0))
    kdim = CMP_BLOCK * HEAD_DIM
    out = pl.BlockSpec((1, nc, 256), lambda b: (b, 0, 0))
    shp = jax.ShapeDtypeStruct((nseq, nc, 256), BF)
    return pl.pallas_call(
        functools.partial(_cmpfin_body, nc=nc), grid=(nseq,),
        in_specs=[ab, ab, _const_spec((8, kdim)), _const_spec((kdim, CMP_HIDDEN)), _const_spec((CMP_HIDDEN, LANES)),
                  _const_spec((8, kdim)), _const_spec((kdim, CMP_HIDDEN)), _const_spec((CMP_HIDDEN, LANES)),
                  _const_spec((LANES, LANES)), _const_spec((1, LANES)), _const_spec((nc, LANES)), _const_spec((nc, LANES))],
        out_specs=[out, out], out_shape=[shp, shp],
        compiler_params=_cparams("parallel"), name="cmp_fin")(abk, abv, pek, w1k, w2k, pev, w1v, w2v, seg, gk, cosc, sinc)


def _cmpattn_body(seq_ref, pos_ref, q_ref, kc_ref, vc_ref, m_ref, oc_ref, sb_ref, *, nq, nc, nl, nblk, k_pick):
    del seq_ref
    pos0 = pos_ref[pl.program_id(0)]
    q = q_ref[0]
    lane = lax.broadcasted_iota(jnp.int32, (nq, LANES), 1)
    low = lane < HEAD_DIM
    trow = pos0 + lax.broadcasted_iota(jnp.int32, (nq, 1), 0)
    t4 = jnp.concatenate([trow] * 4, axis=0)
    posc = CMP_STRIDE * lax.broadcasted_iota(jnp.int32, (1, nc), 1) + (CMP_BLOCK - 1)
    valid = posc <= t4
    imps = []
    for g in range(NSA_KV_HEADS):
        rows = []
        for hh in range(4):
            h = 4 * g + hh
            pr = q[:, LANES * (h // 2): LANES * (h // 2 + 1)]
            rows.append(jnp.where(low, pr, 0.0) if h % 2 == 0 else jnp.where(low, 0.0, pr))
        qg = (jnp.concatenate(rows, axis=0) * LOG2E).astype(BF)
        s = lax.dot_general(qg, kc_ref[0, :, LANES * g: LANES * (g + 1)], NT, preferred_element_type=F32)
        s = jnp.where(valid, s, NEG)
        m = jnp.max(s, axis=1, keepdims=True)
        e = jnp.exp2(s - m)
        p = e * jnp.where(m > 0.5 * NEG, 1.0 / jnp.sum(e, axis=1, keepdims=True), 0.0)
        o = jnp.dot(p.astype(BF), vc_ref[0, :, LANES * g: LANES * (g + 1)], preferred_element_type=F32)
        for pj in range(2):
            oc_ref[0, :, 256 * g + LANES * pj: 256 * g + LANES * (pj + 1)] = jnp.where(
                low, o[2 * pj * nq: (2 * pj + 1) * nq], o[(2 * pj + 1) * nq: (2 * pj + 2) * nq])
        imps.append(p[0:nq] + p[nq:2 * nq] + p[2 * nq:3 * nq] + p[3 * nq:4 * nq])

    nr = max(2 * nq, LANES)
    imp = jnp.concatenate(imps + ([jnp.zeros((nr - 2 * nq, nc), F32)] if nr > 2 * nq else []), axis=0)
    mt = m_ref[...]
    ps = sum(lax.dot_general(mt, part, NT, preferred_element_type=F32) for part in _split3(imp))
    blk = lax.broadcasted_iota(jnp.int32, (nl, 1), 0)
    blkf = blk.astype(F32)
    tcol = pos0 + lax.broadcasted_iota(jnp.int32, (1, nr), 1) % nq
    cur = tcol // SEL_BLOCK
    force = (blk == 0) | (blk == cur) | (blk == cur - 1)
    score = jnp.where(force, FORCE_SCORE, jnp.where(blk * SEL_BLOCK <= tcol, ps, -1.0))
    score = jnp.where(blk < nblk, score, -1.0)

    def pick(_, sc):
        mx = jnp.max(sc, axis=0, keepdims=True)
        idx = jnp.min(jnp.where(sc == mx, blkf, 1e9), axis=0, keepdims=True)
        return jnp.where(blkf == idx, -2.0, sc)

    picked = lax.fori_loop(0, k_pick, pick, score) == -2.0
    sel_rows = jnp.where(picked, 0.0, SEL_OFF).T
    for g in range(NSA_KV_HEADS):
        sb_ref[0, :, nl * g: nl * (g + 1)] = sel_rows[nq * g: nq * (g + 1)].astype(BF)


def _cmp_attn(q3, kcd, vcd, mmat, seq_of, pos0, nq, nblk, k_pick):
    nb = q3.shape[0]
    nc = kcd.shape[1]
    nl = mmat.shape[0]
    gs = pltpu.PrefetchScalarGridSpec(
        num_scalar_prefetch=2, grid=(nb,),
        in_specs=[pl.BlockSpec((1, nq, 512), lambda b, sq, ps: (b, 0, 0)),
                  pl.BlockSpec((1, nc, 256), lambda b, sq, ps: (sq[b], 0, 0)),
                  pl.BlockSpec((1, nc, 256), lambda b, sq, ps: (sq[b], 0, 0)),
                  pl.BlockSpec((nl, nc), lambda b, sq, ps: (0, 0), pipeline_mode=pl.Buffered(1))],
        out_specs=[pl.BlockSpec((1, nq, 512), lambda b, sq, ps: (b, 0, 0)),
                   pl.BlockSpec((1, nq, 2 * nl), lambda b, sq, ps: (b, 0, 0))])
    return pl.pallas_call(
        functools.partial(_cmpattn_body, nq=nq, nc=nc, nl=nl, nblk=nblk, k_pick=k_pick), grid_spec=gs,
        out_shape=[jax.ShapeDtypeStruct((nb, nq, 512), F32), jax.ShapeDtypeStruct((nb, nq, 2 * nl), BF)],
        compiler_params=_cparams("parallel"), name="cmp_attn")(seq_of, pos0, q3, kcd, vcd, mmat)


def _group_rows(q, low, width_tail=None):
    rows = []
    for hh in range(4):
        pr = q[:, LANES * (hh // 2): LANES * (hh // 2 + 1)]
        rows.append(jnp.where(low, pr, 0.0) if hh % 2 == 0 else jnp.where(low, 0.0, pr))
    return rows


def _store_group(o_ref, o, low, n):
    for pj in range(2):
        o_ref[:, LANES * pj: LANES * (pj + 1)] = jnp.where(low, o[2 * pj * n: (2 * pj + 1) * n],
                                                           o[(2 * pj + 1) * n: (2 * pj + 2) * n])


def _selp_body(qi_ref, kj_ref, q_ref, sb_ref, ka_ref, v_ref, o_ref, qm_ref, m_ref, l_ref, acc_ref, *, tq, rb):
    sid = pl.program_id(1)
    i = qi_ref[sid]
    j = kj_ref[sid]
    low = lax.broadcasted_iota(jnp.int32, (tq, LANES), 1) < HEAD_DIM

    @pl.when(j == 0)
    def _():
        q = q_ref[...] * LOG2E
        for hh in range(4):
            pr = q[:, LANES * (hh // 2): LANES * (hh // 2 + 1)]
            sw = pltpu.roll(pr, HEAD_DIM, 1)
            qm_ref[hh] = (jnp.where(low, pr, sw) if hh % 2 == 0 else jnp.where(low, sw, pr)).astype(BF)
        m_ref[...] = jnp.full_like(m_ref, NEG)
        l_ref[...] = jnp.zeros_like(l_ref)
        acc_ref[...] = jnp.zeros_like(acc_ref)

    bias_half = (lax.broadcasted_iota(jnp.int32, (tq, LANES), 1) // HEAD_DIM) == ((j * tq) // (HEAD_DIM * SEL_BLOCK)) % 2

    def step(diag):
        for hh in range(4):
            for t0 in range(0, tq, rb):
                nk = t0 + rb if diag else tq
                qa = jnp.where(bias_half[t0:t0 + rb], sb_ref[t0:t0 + rb, :], qm_ref[hh, t0:t0 + rb, :])
                s = lax.dot_general(qa, ka_ref[0, 0:nk, :], NT, preferred_element_type=F32)
                if diag:
                    s = _causal_chunk(s, t0)
                _chunk_update(s, v_ref[0, 0:nk, :], m_ref, l_ref, acc_ref, 0, hh * tq + t0, rb)

    pl.when(j < i)(lambda: step(False))

    @pl.when(j == i)
    def _():
        step(True)
        _store_group(o_ref, acc_ref[0] / l_ref[0], low, tq)


def _sel_prompt(qn, selb, kaug, vdup, t, tq, rb, nl):
    n = t // tq
    qi, kj = _tri_steps(n)
    nsup = nl // LANES
    per_sup = LANES * SEL_BLOCK // tq
    gs = pltpu.PrefetchScalarGridSpec(
        num_scalar_prefetch=2, grid=(NSA_KV_HEADS, qi.shape[0]),
        in_specs=[pl.BlockSpec((tq, 256), lambda g, s, qi, kj: (qi[s], g)),
                  pl.BlockSpec((tq, LANES), lambda g, s, qi, kj: (qi[s], g * nsup + kj[s] // per_sup)),
                  pl.BlockSpec((1, tq, LANES), lambda g, s, qi, kj: (g, kj[s], 0)),
                  pl.BlockSpec((1, tq, LANES), lambda g, s, qi, kj: (g, kj[s], 0))],
        out_specs=pl.BlockSpec((tq, 256), lambda g, s, qi, kj: (qi[s], g)),
        scratch_shapes=[pltpu.VMEM((4, tq, LANES), BF), pltpu.VMEM((1, 4 * tq, LANES), F32),
                        pltpu.VMEM((1, 4 * tq, LANES), F32), pltpu.VMEM((1, 4 * tq, LANES), F32)])
    return pl.pallas_call(
        functools.partial(_selp_body, tq=tq, rb=rb), grid_spec=gs,
        out_shape=jax.ShapeDtypeStruct((t, 512), F32),
        compiler_params=_cparams("parallel", "arbitrary"), name="sel_prompt")(qi, kj, qn, selb, kaug, vdup)


def _winp_body(q_ref, *refs, tq, nprev):
    nt = nprev + 1
    k_refs, v_refs, o_ref = refs[:nt], refs[nt:2 * nt], refs[2 * nt]
    i = pl.program_id(1)
    low = lax.broadcasted_iota(jnp.int32, (tq, LANES), 1) < HEAD_DIM
    k = jnp.concatenate([r[0] for r in k_refs], axis=0)
    v = jnp.concatenate([r[0] for r in v_refs], axis=0)
    nk = k.shape[0]
    tpos = i * tq + lax.broadcasted_iota(jnp.int32, (tq, nk), 0)
    kpos = (i - nprev) * tq + lax.broadcasted_iota(jnp.int32, (tq, nk), 1)
    ok = (kpos <= tpos) & (tpos - kpos < WINDOW) & (kpos >= 0)
    outs = []
    for r in _group_rows(q_ref[...] * LOG2E, low):
        s = jnp.where(ok, lax.dot_general(r.astype(BF), k, NT, preferred_element_type=F32), NEG)
        e = jnp.exp2(s - jnp.max(s, axis=1, keepdims=True))
        outs.append(jnp.dot(e.astype(BF), v, preferred_element_type=F32) / jnp.sum(e, axis=1, keepdims=True))
    _store_group(o_ref, jnp.concatenate(outs, axis=0), low, tq)


def _win_prompt(qn, kwdup, vwdup, t, tq):
    nprev = WINDOW // tq
    kv = [pl.BlockSpec((1, tq, LANES), lambda g, i, dd=dd: (g, jnp.maximum(i - nprev + dd, 0), 0)) for dd in range(nprev + 1)]
    return pl.pallas_call(
        functools.partial(_winp_body, tq=tq, nprev=nprev), grid=(NSA_KV_HEADS, t // tq),
        in_specs=[pl.BlockSpec((tq, 256), lambda g, i: (i, g))] + kv + kv,
        out_specs=pl.BlockSpec((tq, 256), lambda g, i: (i, g)),
        out_shape=jax.ShapeDtypeStruct((t, 512), F32),
        compiler_params=_cparams("parallel", "parallel"), name="win_prompt")(
        qn, *([kwdup] * (nprev + 1)), *([vwdup] * (nprev + 1)))


ROWS_S = NSA_HEADS * NQS
ROWS_G = ROWS_S // NSA_KV_HEADS


def _new_key_mask(shape):
    r = lax.broadcasted_iota(jnp.int32, shape, 0) % NQS
    c = lax.broadcasted_iota(jnp.int32, shape, 1)
    return c <= r


def _slab_update(s, pv_of, m_ref, l_ref, acc_ref):
    m_prev = m_ref[0]
    m_new = jnp.maximum(m_prev, jnp.max(s, axis=1, keepdims=True))
    alpha = jnp.exp(m_prev - m_new)
    p = jnp.exp(s - m_new)
    l_ref[0] = alpha * l_ref[0] + jnp.sum(p, axis=1, keepdims=True)
    acc_ref[0] = alpha * acc_ref[0] + pv_of(p)
    m_ref[0] = m_new


def _slab_finish(sn, pv_of, m_ref, l_ref, acc_ref):
    m_prev = m_ref[0]
    m_new = jnp.maximum(m_prev, jnp.max(sn, axis=1, keepdims=True))
    alpha = jnp.exp(m_prev - m_new)
    pn = jnp.exp(sn - m_new)
    l = alpha * l_ref[0] + jnp.sum(pn, axis=1, keepdims=True)
    return (alpha * acc_ref[0] + pv_of(pn)) / l


def _slab_init(m_ref, l_ref, acc_ref):
    m_ref[...] = jnp.full_like(m_ref, NEG)
    l_ref[...] = jnp.zeros_like(l_ref)
    acc_ref[...] = jnp.zeros_like(acc_ref)


def _grp(a, g):
    return a[ROWS_G * g: ROWS_G * (g + 1)]


def _sels_body(pt_ref, q_ref, sb0_ref, sb1_ref, e_ref, kn_ref, vn_ref, *refs, pps, nsteps, per_chunk):
    del pt_ref
    pages = refs[:pps]
    o_ref, m_ref, l_ref, acc_ref = refs[pps:]
    sid = pl.program_id(1)
    pl.when(sid == 0)(lambda: _slab_init(m_ref, l_ref, acc_ref))

    q = q_ref[0] * SCALE
    qb = q.astype(BF)
    slab = lambda kind, g: jnp.concatenate([r[0, kind, g] for r in pages], axis=1).astype(BF)
    sbrows = jnp.concatenate([sb0_ref[0].astype(F32)] * 4 + [sb1_ref[0].astype(F32)] * 4, axis=0).astype(BF)
    bias = jnp.dot(sbrows, e_ref[sid % per_chunk], preferred_element_type=F32)
    s = jnp.concatenate([jnp.dot(_grp(qb, g), slab(0, g), preferred_element_type=F32)
                         for g in range(NSA_KV_HEADS)], axis=0) + bias
    _slab_update(s, lambda p: jnp.concatenate(
        [lax.dot_general(_grp(p, g).astype(BF), slab(1, g), NT, preferred_element_type=F32)
         for g in range(NSA_KV_HEADS)], axis=0), m_ref, l_ref, acc_ref)

    @pl.when(sid == nsteps - 1)
    def _():
        sn = jnp.concatenate([lax.dot_general(_grp(q, g), kn_ref[0, g], NT, preferred_element_type=F32)
                              for g in range(NSA_KV_HEADS)], axis=0)
        sn = jnp.where(_new_key_mask(sn.shape), sn, NEG)
        o_ref[0] = _slab_finish(sn, lambda pn: jnp.concatenate(
            [jnp.dot(_grp(pn, g), vn_ref[0, g], preferred_element_type=F32) for g in range(NSA_KV_HEADS)], axis=0),
            m_ref, l_ref, acc_ref)


def _sel_sample(qh, selb, kn, vn, pool_t, pt_flat, eexp, b, npg, pps, nl):
    nsteps = npg // pps
    per_chunk = eexp.shape[0]
    nchunk = nl // LANES
    pg = [pl.BlockSpec((1, 2, 2, HEAD_DIM, PAGE), lambda bb, s, pt, k=k: (pt[bb * npg + s * pps + k], 1, 0, 0, 0))
          for k in range(pps)]
    row = pl.BlockSpec((1, ROWS_S, HEAD_DIM), lambda bb, s, pt: (bb, 0, 0))
    new = pl.BlockSpec((1, NSA_KV_HEADS, NQS, HEAD_DIM), lambda bb, s, pt: (bb, 0, 0, 0))
    gs = pltpu.PrefetchScalarGridSpec(
        num_scalar_prefetch=1, grid=(b, nsteps),
        in_specs=[row,
                  pl.BlockSpec((1, NQS, LANES), lambda bb, s, pt: (bb, 0, s // per_chunk)),
                  pl.BlockSpec((1, NQS, LANES), lambda bb, s, pt: (bb, 0, nchunk + s // per_chunk)),
                  pl.BlockSpec(eexp.shape, lambda bb, s, pt: (0, 0, 0), pipeline_mode=pl.Buffered(1)),
                  new, new] + pg,
        out_specs=row,
        scratch_shapes=[pltpu.VMEM((1, ROWS_S, 1), F32), pltpu.VMEM((1, ROWS_S, 1), F32),
                        pltpu.VMEM((1, ROWS_S, HEAD_DIM), F32)])
    return pl.pallas_call(
        functools.partial(_sels_body, pps=pps, nsteps=nsteps, per_chunk=per_chunk), grid_spec=gs,
        out_shape=jax.ShapeDtypeStruct((b, ROWS_S, HEAD_DIM), F32),
        compiler_params=_cparams("parallel", "arbitrary"), name="sel_sample")(
        pt_flat, qh, selb, selb, eexp, kn, vn, *([pool_t] * pps))


def _wins_body(q_ref, st_ref, kn_ref, vn_ref, o_ref, *, wb):
    q = q_ref[0] * SCALE
    qb = q.astype(BF)
    s = jnp.concatenate([jnp.dot(_grp(qb, g), st_ref[0, 0, g].astype(BF), preferred_element_type=F32)
                         for g in range(NSA_KV_HEADS)], axis=0)
    tq = lax.broadcasted_iota(jnp.int32, s.shape, 0) % NQS
    n = lax.broadcasted_iota(jnp.int32, s.shape, 1)
    s = jnp.where(n > tq + (wb - WINDOW), s, NEG)
    sn = jnp.concatenate([lax.dot_general(_grp(q, g), kn_ref[0, g], NT, preferred_element_type=F32)
                          for g in range(NSA_KV_HEADS)], axis=0)
    sn = jnp.where(_new_key_mask(sn.shape), sn, NEG)
    m = jnp.maximum(jnp.max(s, axis=1, keepdims=True), jnp.max(sn, axis=1, keepdims=True))
    e = jnp.exp(s - m)
    en = jnp.exp(sn - m)
    den = jnp.sum(e, axis=1, keepdims=True) + jnp.sum(en, axis=1, keepdims=True)
    o = jnp.concatenate(
        [lax.dot_general(_grp(e, g).astype(BF), st_ref[0, 1, g].astype(BF), NT, preferred_element_type=F32)
         + jnp.dot(_grp(en, g), vn_ref[0, g], preferred_element_type=F32) for g in range(NSA_KV_HEADS)], axis=0)
    o_ref[0] = o / den


def _win_sample(qh, state_t, kn, vn, b, wb):
    row = pl.BlockSpec((1, ROWS_S, HEAD_DIM), lambda bb: (bb, 0, 0))
    new = pl.BlockSpec((1, NSA_KV_HEADS, NQS, HEAD_DIM), lambda bb: (bb, 0, 0, 0))
    return pl.pallas_call(
        functools.partial(_wins_body, wb=wb), grid=(b,),
        in_specs=[row, pl.BlockSpec((1, 2, NSA_KV_HEADS, HEAD_DIM, wb), lambda bb: (bb, 0, 0, 0, 0)), new, new],
        out_specs=row, out_shape=jax.ShapeDtypeStruct((b, ROWS_S, HEAD_DIM), F32),
        compiler_params=_cparams("parallel"), name="win_sample")(qh, state_t, kn, vn)


def _foxs_body(pt_ref, q_ref, kn_ref, vn_ref, lfn_ref, u_ref, *refs, pps, nsteps):
    del pt_ref
    kv = refs[:pps]
    lfp = refs[pps:2 * pps]
    o_ref, m_ref, l_ref, acc_ref, carry_ref = refs[2 * pps:]
    sid = pl.program_id(1)

    @pl.when(sid == 0)
    def _():
        _slab_init(m_ref, l_ref, acc_ref)
        carry_ref[...] = jnp.zeros_like(carry_ref)

    q = q_ref[0] * SCALE
    qb = q.astype(BF)
    head = lambda a, h: a[NQS * h: NQS * (h + 1)]
    keys = u_ref.shape[0]
    cpp = keys // PAGE
    for c0 in range(0, pps, cpp):
        kvc, lfc = kv[c0:c0 + cpp], lfp[c0:c0 + cpp]
        slab = lambda kind, h: jnp.concatenate([r[0, kind, h] for r in kvc], axis=1).astype(BF)
        lft = jnp.concatenate([r[0] for r in lfc], axis=1)
        cum = _dot3(lft, u_ref[...]) + carry_ref[:, 0:1]
        carry_ref[...] = jnp.broadcast_to(cum[:, keys - 1:], carry_ref.shape)
        s = jnp.concatenate([jnp.dot(head(qb, h), slab(0, h), preferred_element_type=F32)
                             - jnp.broadcast_to(cum[h:h + 1, :], (NQS, keys)) for h in range(FOX_HEADS)], axis=0)
        _slab_update(s, lambda p: jnp.concatenate(
            [lax.dot_general(head(p, h).astype(BF), slab(1, h), NT, preferred_element_type=F32)
             for h in range(FOX_HEADS)], axis=0), m_ref, l_ref, acc_ref)

    @pl.when(sid == nsteps - 1)
    def _():
        lfn = lfn_ref[0]
        col = lax.broadcasted_iota(jnp.int32, (FOX_HEADS, NQS), 1)
        cn = carry_ref[:, 0:NQS]
        for i in range(NQS):
            cn = cn + jnp.where(col >= i, lfn[:, i:i + 1], 0.0)
        sn = jnp.concatenate([lax.dot_general(head(q, h), kn_ref[0, h], NT, preferred_element_type=F32)
                              - jnp.broadcast_to(cn[h:h + 1, :], (NQS, NQS)) for h in range(FOX_HEADS)], axis=0)
        sn = jnp.where(_new_key_mask(sn.shape), sn, NEG)
        o_ref[0] = _slab_finish(sn, lambda pn: jnp.concatenate(
            [jnp.dot(head(pn, h), vn_ref[0, h], preferred_element_type=F32) for h in range(FOX_HEADS)], axis=0),
            m_ref, l_ref, acc_ref)


def _fox_sample(qh, kn, vn, lfn, pool_t, lf_t, pt_flat, b, npg, pps):
    nsteps = npg // pps
    keys = math.gcd(pps, 8) * PAGE
    u = jnp.asarray(np.triu(np.ones((keys, keys), np.float32)), BF)
    kvs = [pl.BlockSpec((1, 2, FOX_HEADS, HEAD_DIM, PAGE), lambda bb, s, pt, k=k: (pt[bb * npg + s * pps + k], 0, 0, 0, 0))
           for k in range(pps)]
    lfs = [pl.BlockSpec((1, FOX_HEADS, PAGE), lambda bb, s, pt, k=k: (pt[bb * npg + s * pps + k], 0, 0)) for k in range(pps)]
    row = pl.BlockSpec((1, ROWS_S, HEAD_DIM), lambda bb, s, pt: (bb, 0, 0))
    new = pl.BlockSpec((1, FOX_HEADS, NQS, HEAD_DIM), lambda bb, s, pt: (bb, 0, 0, 0))
    gs = pltpu.PrefetchScalarGridSpec(
        num_scalar_prefetch=1, grid=(b, nsteps),
        in_specs=[row, new, new, pl.BlockSpec((1, FOX_HEADS, NQS), lambda bb, s, pt: (bb, 0, 0)),
                  pl.BlockSpec((keys, keys), lambda bb, s, pt: (0, 0), pipeline_mode=pl.Buffered(1))] + kvs + lfs,
        out_specs=row,
        scratch_shapes=[pltpu.VMEM((1, ROWS_S, 1), F32), pltpu.VMEM((1, ROWS_S, 1), F32),
                        pltpu.VMEM((1, ROWS_S, HEAD_DIM), F32), pltpu.VMEM((FOX_HEADS, LANES), F32)])
    return pl.pallas_call(
        functools.partial(_foxs_body, pps=pps, nsteps=nsteps), grid_spec=gs,
        out_shape=jax.ShapeDtypeStruct((b, ROWS_S, HEAD_DIM), F32),
        compiler_params=_cparams("parallel", "arbitrary"), name="fox_sample")(
        pt_flat, qh, kn, vn, lfn, u, *([pool_t] * pps), *([lf_t] * pps))


def _rope_tables(pos):
    half = ROT_DIM // 2
    inv = ROPE_THETA ** (-(jnp.arange(half, dtype=F32) * 2.0 / ROT_DIM))
    ang = pos.astype(F32)[:, None] * inv[None, :]
    cos, sin = jnp.cos(ang), jnp.sin(ang)
    n = pos.shape[0]
    c64 = jnp.concatenate([cos, cos, jnp.ones((n, HEAD_DIM - ROT_DIM), F32)], axis=1)
    s64 = jnp.concatenate([-sin, sin, jnp.zeros((n, HEAD_DIM - ROT_DIM), F32)], axis=1)
    return jnp.tile(c64, (1, 2)), jnp.tile(s64, (1, 2))


def _importance_matrix(nc, nl):
    m = np.zeros((nl, nc), np.float32)
    per = SEL_BLOCK // CMP_STRIDE
    for n in range(nc - 1):
        m[n // per, n] += 1.0
        m[(n + 1) // per, n] += 1.0
    return jnp.asarray(m, BF)


def _gate_expand():
    e = np.zeros((3, LANES, 512), np.float32)
    for i in range(3):
        for h in range(NSA_HEADS):
            e[i, 3 * h + i, HEAD_DIM * h: HEAD_DIM * (h + 1)] = 1.0
    return jnp.asarray(e, BF)


def _block_expand(pps):
    per_step = pps * PAGE // SEL_BLOCK
    per_chunk = LANES // per_step
    e = np.zeros((per_chunk, LANES, pps * PAGE), np.float32)
    for r in range(per_chunk):
        for key in range(pps * PAGE):
            e[r, r * per_step + key // SEL_BLOCK, key] = 1.0
    return jnp.asarray(e, BF)


def _cmp_weights(w1):
    top, bot = w1[:CMP_STRIDE], w1[CMP_STRIDE:]
    wl = jnp.concatenate([top, bot], axis=2)
    z = jnp.zeros_like(wl)
    w = jnp.concatenate([jnp.concatenate([wl, z], axis=2), jnp.concatenate([z, wl], axis=2)], axis=1)
    return w.reshape(CMP_STRIDE * LANES, 512).astype(BF)


def _pad_rows(a, rp):
    return jnp.pad(a, ((0, rp - a.shape[0]),) + ((0, 0),) * (a.ndim - 1))


def kernel(x_prompt, x_sample, cache_nsa_kv, cache_fox_kv, cache_fox_logf, state_nsa_win_kv, page_table, norm_ffn1, w1_gate, w1_up, w1_down, norm_mix, w_in, b_forget, nsa_q_norm, nsa_k_norm, cmp_pe_k, cmp_w1_k, cmp_w2_k, cmp_pe_v, cmp_w1_v, cmp_w2_v, fox_q_norm, fox_k_norm, w_branch_nsa, w_branch_fox, w_out, norm_ffn2, w2_gate, w2_up, w2_down):
    bp, t, d = x_prompt.shape
    b, ts, _ = x_sample.shape
    assert bp == 1 and norm_ffn1.shape[0] == 1 and ts <= 4
    npg = page_table.shape[1]
    past = npg * PAGE
    n_pool = cache_nsa_kv.shape[1]
    wb = state_nsa_win_kv.shape[2]
    tm = 256
    tq_fox = min(1024, t)
    tq_sel = min(1024, t)
    rb = 1024
    tq_win = 512
    nq_cmp = 128
    pps_cmp = math.gcd(math.gcd(t // PAGE, npg), 32)
    pps_sel, pps_fox = math.gcd(npg, 32), math.gcd(npg, 16)
    assert t % PAGE == 0 and t % tq_fox == 0 and t % tq_sel == 0 and t % tq_win == 0 and WINDOW % tq_win == 0
    assert (HEAD_DIM * SEL_BLOCK) % tq_sel == 0 and LANES % (2 * pps_sel) == 0

    rs = b * NQS
    tm_s = tm if rs >= tm else rs
    rsp = -(-rs // tm_s) * tm_s
    assert t % tm == 0
    xs = _pad_rows(jnp.pad(x_sample, ((0, 0), (0, NQS - ts), (0, 0))).reshape(rs, d), rsp)
    pos_s = jnp.concatenate([past + (jnp.arange(rs) % NQS), jnp.zeros((rsp - rs,), jnp.int32)])

    f = w1_gate.shape[2]
    fp = -(-f // LANES) * LANES
    padc = lambda w: jnp.pad(w.astype(BF), ((0, 0), (0, fp - f)))
    padr = lambda w: jnp.pad(w.astype(BF), ((0, fp - f), (0, 0)))
    wi = w_in[0]
    o2, o3, o4, o5 = 1280, 1304, 2840, 2848
    wa = jnp.concatenate([wi[:, :o2], wi[:, o3:o4], wi[:, o5:], wi[:, o2:o3], wi[:, o4:o5],
                          jnp.zeros((d, LANES - 32), F32)], axis=1).astype(BF)
    bm = jnp.zeros((1, LANES), F32).at[0, 24:32].set(b_forget[0])
    g2 = lambda g: jnp.tile(g, 2)[None, :]
    seg = jnp.asarray(np.kron(np.eye(2, dtype=np.float32), np.full((HEAD_DIM, HEAD_DIM), 1.0 / HEAD_DIM, np.float32)), BF)

    ffn1_w = (norm_ffn1, padc(w1_gate[0]), padc(w1_up[0]), padr(w1_down[0]))
    ffn2_w = (norm_ffn2, padc(w2_gate[0]), padc(w2_up[0]), padr(w2_down[0]))
    proj_w = (norm_mix, wa, seg, g2(nsa_q_norm[0]), g2(nsa_k_norm[0, 1]), g2(nsa_k_norm[0, 2]), g2(fox_q_norm[0]),
              g2(fox_k_norm[0]), bm)

    def front(x, pos, tmx):
        x1 = _ffn(x, *ffn1_w, tmx)
        return (x1,) + tuple(_proj(x1, *proj_w, *_rope_tables(pos), tmx))

    (x1, qn, nsa_rows, win_rows, qf, fox_rows, gm, misc, foxb, kaug, vdup, kwdup, vwdup) = front(
        x_prompt[0], jnp.arange(t), tm)
    (x1_s, qn_n, nsa_new, win_new_r, qf_n, fox_new, gm_s, misc_s) = front(xs, pos_s, tm_s)[:8]

    ct = _cumsum_t(misc, t, min(512, t)).reshape(FOX_HEADS // 2, 2, t)
    o_fox_p = _fox_prompt(qf, foxb, ct, t, tq_fox, rb)

    wk, wv = _cmp_weights(cmp_w1_k[0]), _cmp_weights(cmp_w1_v[0])
    kdim = CMP_BLOCK * HEAD_DIM
    pe8 = lambda pe: jnp.broadcast_to(pe.reshape(1, kdim), (8, kdim))
    w2d = lambda w2: jnp.concatenate([w2, w2], axis=1).astype(BF)
    gkc = g2(nsa_k_norm[0, 0])

    def compress(ab_fn, pool, pt_flat, nseq, nc):
        abk, abv = ab_fn(pool, pt_flat, wk, wv, pps_cmp)
        posc = jnp.arange(nc) * CMP_STRIDE + (CMP_BLOCK - 1)
        cosc, sinc = _rope_tables(posc)
        return _cmp_fin(abk, abv, pe8(cmp_pe_k[0]), cmp_w1_k[0].reshape(kdim, CMP_HIDDEN), w2d(cmp_w2_k[0]),
                        pe8(cmp_pe_v[0]), cmp_w1_v[0].reshape(kdim, CMP_HIDDEN), w2d(cmp_w2_v[0]), seg, gkc, cosc, sinc,
                        nseq, nc)

    nsub = PAGE // CMP_STRIDE
    tpg = t // PAGE
    kc_p, vc_p = compress(_cmp_ab, nsa_rows.reshape(tpg, PAGE, 512), jnp.arange(tpg, dtype=jnp.int32), 1, tpg * nsub)
    nsa_t = jnp.transpose(cache_nsa_kv[0], (0, 2, 3, 4, 1))
    fox_t = jnp.transpose(cache_fox_kv[0], (0, 2, 3, 4, 1))
    lf_t = jnp.transpose(cache_fox_logf[0], (0, 2, 1))
    st_t = jnp.transpose(state_nsa_win_kv[0], (0, 2, 3, 4, 1))
    pt_flat = page_table.reshape(-1).astype(jnp.int32)
    kc_s, vc_s = compress(_cmp_ab_t, nsa_t, pt_flat, b, npg * nsub)

    nl_p = -(-(t // SEL_BLOCK) // LANES) * LANES
    nl_s = -(-(past // SEL_BLOCK) // LANES) * LANES
    nbp = t // nq_cmp
    oc_p, sb_p = _cmp_attn(qn.reshape(nbp, nq_cmp, 512), kc_p, vc_p, _importance_matrix(tpg * nsub, nl_p),
                           jnp.zeros((nbp,), jnp.int32), jnp.arange(nbp, dtype=jnp.int32) * nq_cmp, nq_cmp,
                           t // SEL_BLOCK, min(N_SEL, t // SEL_BLOCK))
    new = lambda a: a[:rs]
    qn_s = new(qn_n).reshape(b, NQS, 512)
    oc_s, sb_s = _cmp_attn(qn_s, kc_s, vc_s, _importance_matrix(npg * nsub, nl_s), jnp.arange(b, dtype=jnp.int32),
                           jnp.full((b,), past, jnp.int32), NQS, past // SEL_BLOCK,
                           min(N_SEL, past // SEL_BLOCK + 1) - 1)

    os_p = _sel_prompt(qn, sb_p.reshape(t, 2 * nl_p), kaug, vdup, t, tq_sel, rb, nl_p)
    ow_p = _win_prompt(qn, kwdup, vwdup, t, tq_win)

    hm = lambda a, nh: new(a).reshape(b, NQS, nh, HEAD_DIM).transpose(0, 2, 1, 3)
    rows_of = lambda o: _pad_rows(o.reshape(b, NSA_HEADS, NQS, HEAD_DIM).transpose(0, 2, 1, 3).reshape(rs, 512), rsp)
    qn_h = hm(qn_n, NSA_HEADS).reshape(b, ROWS_S, HEAD_DIM)
    os_s = _sel_sample(qn_h, sb_s, hm(nsa_new[:, 256:384], NSA_KV_HEADS), hm(nsa_new[:, 384:512], NSA_KV_HEADS), nsa_t,
                       pt_flat, _block_expand(pps_sel), b, npg, pps_sel, nl_s)
    ow_s = _win_sample(qn_h, st_t, hm(win_new_r[:, 0:128], NSA_KV_HEADS), hm(win_new_r[:, 128:256], NSA_KV_HEADS), b, wb)
    of_s = _fox_sample(hm(qf_n, FOX_HEADS).reshape(b, ROWS_S, HEAD_DIM), hm(fox_new[:, 0:512], FOX_HEADS),
                       hm(fox_new[:, 512:1024], FOX_HEADS), new(misc_s)[:, 24:32].reshape(b, NQS, FOX_HEADS).transpose(0, 2, 1),
                       fox_t, lf_t, pt_flat, b, npg, pps_fox)

    merge_w = (_gate_expand(), w_branch_nsa[0].astype(BF), w_branch_fox[0].astype(BF), w_out[0].astype(BF))
    y = _ffn(_merge(x1, oc_p.reshape(t, 512), os_p, ow_p, o_fox_p, misc, gm, *merge_w, tm), *ffn2_w, tm)
    y_s = _ffn(_merge(x1_s, _pad_rows(oc_s.reshape(rs, 512), rsp), rows_of(os_s), rows_of(ow_s), rows_of(of_s), misc_s,
                      gm_s, *merge_w, tm_s), *ffn2_w, tm_s)

    smp = lambda a, *shape: new(a).reshape((b, NQS) + shape)[:, :ts]
    g, hd = NSA_KV_HEADS, HEAD_DIM
    win_new = smp(win_new_r, 2, g, hd)
    win_all_s = jnp.concatenate([state_nsa_win_kv[0], win_new], axis=1)
    keep_p = min(WINDOW, t)
    keep_s = min(WINDOW, wb + ts)
    return (y[None], smp(y_s, d),
            nsa_rows.reshape(1, 1, t, 4, g, hd), smp(nsa_new, 4, g, hd)[None],
            fox_rows.reshape(1, 1, t, 2, FOX_HEADS, hd), smp(fox_new, 2, FOX_HEADS, hd)[None],
            misc[:, 24:32].reshape(1, 1, t, FOX_HEADS), smp(misc_s[:, 24:32], FOX_HEADS)[None],
            win_rows[t - keep_p:].reshape(1, 1, keep_p, 2, g, hd), win_all_s[:, wb + ts - keep_s:][None])
```
